```python
import jax, jax.numpy as jnp
from jax import lax
import numpy as np

D_MODEL = 2048
BATCH = 4
SEQ = 2048
DEPTH = 4

CHUNK = 64
D_MIX = D_MODEL
D_GLA = D_MIX // 2
D_ATT = D_MIX - D_GLA
GLA_HEADS = 4
GLA_DK = D_GLA // 2 // GLA_HEADS
GLA_DV = D_GLA // GLA_HEADS
GLA_KW = GLA_HEADS * GLA_DK
GLA_GATE_RANK = 16
GLA_TAU = 16.0
ATT_HEADS = 8
ATT_HD = D_ATT // ATT_HEADS
LEFT_CHUNKS = 8
BAND = (LEFT_CHUNKS + 1) * CHUNK
REL_CLIP = 128
N_REL = 2 * REL_CLIP + 1
EPS = 1e-6

SPLIT_SIZES = (GLA_KW, GLA_KW, D_GLA, D_GLA, GLA_GATE_RANK, D_ATT, D_ATT, D_ATT, D_ATT)
D_IN = GLA_KW * 2 + D_GLA * 2 + GLA_GATE_RANK + D_ATT * 4

kernel_name = "hymba_gla_chunkattn_sandwich"


def rmsnorm(x, g):
    xf = x.astype(jnp.float32)
    y = xf * lax.rsqrt(jnp.mean(xf * xf, axis=-1, keepdims=True) + EPS) * g.astype(jnp.float32)
    return y.astype(x.dtype)


def split_columns(z):
    points = []
    acc = 0
    for s in SPLIT_SIZES[:-1]:
        acc += s
        points.append(acc)
    return jnp.split(z, points, axis=-1)


def gla_chunk_causal(q, k, v, log_a):
    out_dtype = v.dtype
    B, S, H, DK = q.shape
    DV = v.shape[-1]
    nc = S // CHUNK
    qf = q.astype(jnp.float32).reshape(B, nc, CHUNK, H, DK) * (DK ** -0.5)
    kf = k.astype(jnp.float32).reshape(B, nc, CHUNK, H, DK)
    vf = v.astype(jnp.float32).reshape(B, nc, CHUNK, H, DV)
    L = jnp.cumsum(log_a.astype(jnp.float32).reshape(B, nc, CHUNK, H, DK), axis=2)
    L_end = L[:, :, -1]
    k_dec = kf * jnp.exp(L_end[:, :, None] - L)
    U = jnp.einsum('bnchk,bnchv->bnhkv', k_dec, vf)
    A = jnp.exp(L_end)

    def step(state, inp):
        a, u = inp
        new = a[..., None] * state + u
        return new, new

    init = jnp.zeros((B, H, DK, DV), jnp.float32)
    _, states = lax.scan(step, init, (jnp.swapaxes(A, 0, 1), jnp.swapaxes(U, 0, 1)))
    states = jnp.swapaxes(states, 0, 1)
    o = jnp.einsum('bnchk,bnhkv->bnchv', qf, states)
    return o.reshape(B, S, H, DV).astype(out_dtype)


def chunk_band_attention(q, k, v, rel_bias):
    B, S, H, D = q.shape
    nc = S // CHUNK
    qc = q.reshape(B, nc, CHUNK, H, D)
    pad = ((0, 0), (LEFT_CHUNKS * CHUNK, 0), (0, 0), (0, 0))
    kp = jnp.pad(k, pad).reshape(B, nc + LEFT_CHUNKS, CHUNK, H, D)
    vp = jnp.pad(v, pad).reshape(B, nc + LEFT_CHUNKS, CHUNK, H, D)
    band_idx = jnp.arange(nc)[:, None] + jnp.arange(LEFT_CHUNKS + 1)[None, :]
    kb = kp[:, band_idx].reshape(B, nc, BAND, H, D)
    vb = vp[:, band_idx].reshape(B, nc, BAND, H, D)
    scores = jnp.einsum('bnqhd,bnkhd->bnhqk', qc, kb,
                        preferred_element_type=jnp.float32) * (D ** -0.5)
    qi = jnp.arange(CHUNK)[:, None] + LEFT_CHUNKS * CHUNK
    kj = jnp.arange(BAND)[None, :]
    rel = jnp.clip(qi - kj, -REL_CLIP, REL_CLIP) + REL_CLIP
    bias = rel_bias.astype(jnp.float32)[:, rel]
    key_chunk = band_idx - LEFT_CHUNKS
    valid = jnp.repeat(key_chunk >= 0, CHUNK, axis=1)
    scores = jnp.where(valid[None, :, None, None, :], scores + bias[None, None], -jnp.inf)
    p = jax.nn.softmax(scores, axis=-1).astype(v.dtype)
    o = jnp.einsum('bnhqk,bnkhd->bnqhd', p, vb)
    return o.reshape(B, S, H, D)


def hybrid_layer(x, w_in, w_out, g_pre, g_post, w_alpha, b_alpha, g_gla, g_att, rel_bias):
    B, S, _ = x.shape
    h = rmsnorm(x, g_pre)
    z = h @ w_in
    gq, gk, gv, gg, ga, aq, ak, av, ag = split_columns(z)
    log_a = jax.nn.log_sigmoid((ga @ w_alpha + b_alpha).astype(jnp.float32)) / GLA_TAU
    o_gla = gla_chunk_causal(gq.reshape(B, S, GLA_HEADS, GLA_DK),
                             gk.reshape(B, S, GLA_HEADS, GLA_DK),
                             gv.reshape(B, S, GLA_HEADS, GLA_DV),
                             log_a.reshape(B, S, GLA_HEADS, GLA_DK))
    o_gla = rmsnorm(o_gla, g_gla.reshape(GLA_HEADS, GLA_DV)).reshape(B, S, D_GLA)
    o_gla = o_gla * jax.nn.silu(gg)
    o_att = chunk_band_attention(aq.reshape(B, S, ATT_HEADS, ATT_HD),
                                 ak.reshape(B, S, ATT_HEADS, ATT_HD),
                                 av.reshape(B, S, ATT_HEADS, ATT_HD), rel_bias)
    o_att = rmsnorm(o_att, g_att.reshape(ATT_HEADS, ATT_HD)).reshape(B, S, D_ATT)
    o_att = o_att * jax.nn.silu(ag)
    y = jnp.concatenate([o_gla, o_att], axis=-1) @ w_out
    return x + rmsnorm(y, g_post)


def setup_inputs(seed: int = 0) -> dict:
    key = jax.random.key(seed)
    ks = jax.random.split(key, 10)
    x = jax.random.normal(ks[0], (BATCH, SEQ, D_MODEL), jnp.float32)
    w_in = jax.random.normal(ks[1], (DEPTH, D_MODEL, D_IN), jnp.float32) * (D_MODEL ** -0.5)
    w_out = jax.random.normal(ks[2], (DEPTH, D_MIX, D_MODEL), jnp.float32) * (D_MIX ** -0.5)
    g_pre = 1.0 + 0.02 * jax.random.normal(ks[3], (DEPTH, D_MODEL), jnp.float32)
    g_post = 1.0 + 0.02 * jax.random.normal(ks[4], (DEPTH, D_MODEL), jnp.float32)
    w_alpha = jax.random.normal(ks[5], (DEPTH, GLA_GATE_RANK, GLA_KW), jnp.float32) * (GLA_GATE_RANK ** -0.5)
    b_alpha = 0.1 * jax.random.normal(ks[6], (DEPTH, GLA_KW), jnp.float32)
    g_gla = 1.0 + 0.02 * jax.random.normal(ks[7], (DEPTH, D_GLA), jnp.float32)
    g_att = 1.0 + 0.02 * jax.random.normal(ks[8], (DEPTH, D_ATT), jnp.float32)
    rel_bias = 0.1 * jax.random.normal(ks[9], (DEPTH, ATT_HEADS, N_REL), jnp.float32)
    return {"x": x, "w_in": w_in, "w_out": w_out, "g_pre": g_pre, "g_post": g_post,
            "w_alpha": w_alpha, "b_alpha": b_alpha, "g_gla": g_gla, "g_att": g_att,
            "rel_bias": rel_bias}


def reference(x, w_in, w_out, g_pre, g_post, w_alpha, b_alpha, g_gla, g_att, rel_bias):
    h = x
    for l in range(DEPTH):
        h = hybrid_layer(h, w_in[l], w_out[l], g_pre[l], g_post[l], w_alpha[l], b_alpha[l],
                         g_gla[l], g_att[l], rel_bias[l])
    return h
```

```python
import functools

import jax
import jax.numpy as jnp
import numpy as np
from jax import lax
from jax.experimental import pallas as pl
from jax.experimental.pallas import tpu as pltpu

D_MODEL = 2048
CHUNK = 64
D_GLA = 1024
D_ATT = 1024
GLA_HEADS = 4
GLA_DK = 128
GLA_DV = 256
GLA_KW = GLA_HEADS * GLA_DK
GATE_RANK = 16
GLA_TAU = 16.0
ATT_HEADS = 8
ATT_HD = 128
LEFT_CHUNKS = 8
REL_CLIP = 128
EPS = 1e-6

LANES = 128
GA_PAD = LANES
D_Z = 2 * GLA_KW + 2 * D_GLA + 4 * D_ATT

Z_GQ, Z_GK, Z_GV, Z_GG = 0, GLA_KW, 2 * GLA_KW, 2 * GLA_KW + D_GLA
Z_AQ = 2 * GLA_KW + 2 * D_GLA
Z_AK, Z_AV, Z_AG = Z_AQ + D_ATT, Z_AQ + 2 * D_ATT, Z_AQ + 3 * D_ATT

IN_TM, IN_TN = 1024, 512
OUT_TM = 256
GLA_GROUP = 256
ATT_TQ = 128
ATT_WIN = ATT_TQ + LEFT_CHUNKS * CHUNK
VMEM_LIMIT = 48 * 1024 * 1024


def _silu(x):
    return x * (1.0 / (1.0 + jnp.exp(-x)))


def _log_sigmoid(x):
    return jnp.minimum(x, 0.0) - jnp.log1p(jnp.exp(-jnp.abs(x)))


def _in_proj_kernel(x_ref, gpre_ref, w_ref, wga_ref, walpha_ref, balpha_ref, z_ref, la_ref, h_ref):
    j = pl.program_id(1)

    @pl.when(j == 0)
    def _():
        x = x_ref[...]
        ms = jnp.mean(x * x, axis=-1, keepdims=True)
        h = (x * lax.rsqrt(ms + EPS) * gpre_ref[...]).astype(jnp.bfloat16)
        h_ref[...] = h
        ga = jnp.dot(h, wga_ref[...], preferred_element_type=jnp.float32)
        pre = jnp.dot(ga.astype(jnp.bfloat16), walpha_ref[...],
                      preferred_element_type=jnp.float32) + balpha_ref[...]
        la_ref[...] = _log_sigmoid(pre) * (1.0 / GLA_TAU)

    z_ref[...] = jnp.dot(h_ref[...], w_ref[...], preferred_element_type=jnp.float32).astype(z_ref.dtype)


def _in_proj(x, g_pre, w_main, w_ga, w_alpha, b_alpha):
    m = x.shape[0]
    grid = (m // IN_TM, D_Z // IN_TN)
    return pl.pallas_call(
        _in_proj_kernel,
        grid=grid,
        in_specs=[
            pl.BlockSpec((IN_TM, D_MODEL), lambda i, j: (i, 0)),
            pl.BlockSpec((1, D_MODEL), lambda i, j: (0, 0)),
            pl.BlockSpec((D_MODEL, IN_TN), lambda i, j: (0, j)),
            pl.BlockSpec((D_MODEL, GA_PAD), lambda i, j: (0, 0)),
            pl.BlockSpec((GA_PAD, GLA_KW), lambda i, j: (0, 0)),
            pl.BlockSpec((1, GLA_KW), lambda i, j: (0, 0)),
        ],
        out_specs=[
            pl.BlockSpec((IN_TM, IN_TN), lambda i, j: (i, j)),
            pl.BlockSpec((IN_TM, GLA_KW), lambda i, j: (i, 0)),
        ],
        out_shape=[
            jax.ShapeDtypeStruct((m, D_Z), jnp.bfloat16),
            jax.ShapeDtypeStruct((m, GLA_KW), jnp.float32),
        ],
        scratch_shapes=[pltpu.VMEM((IN_TM, D_MODEL), jnp.bfloat16)],
        compiler_params=pltpu.CompilerParams(
            dimension_semantics=("parallel", "arbitrary"), vmem_limit_bytes=VMEM_LIMIT),
        name="in_proj",
    )(x, g_pre, w_main, w_ga, w_alpha, b_alpha)


def _split3_bf16(x):
    hi = x.astype(jnp.bfloat16)
    r1 = x - hi.astype(jnp.float32)
    mid = r1.astype(jnp.bfloat16)
    lo = (r1 - mid.astype(jnp.float32)).astype(jnp.bfloat16)
    return hi, mid, lo


def _gla_kernel(q_ref, k_ref, v_ref, gate_ref, la_ref, g_ref, o_ref, st_ref):
    seq = q_ref.shape[0]
    n_groups = seq // GLA_GROUP
    chunks_per_group = GLA_GROUP // CHUNK
    r = lax.broadcasted_iota(jnp.int32, (GLA_GROUP, GLA_GROUP), 0)
    c = lax.broadcasted_iota(jnp.int32, (GLA_GROUP, GLA_GROUP), 1)
    tri = jnp.where((r // CHUNK == c // CHUNK) & (c <= r), 1.0, 0.0).astype(jnp.bfloat16)
    st_ref[...] = jnp.zeros_like(st_ref)
    g = g_ref[...]

    def group_step(gi, carry):
        base = pl.multiple_of(gi * GLA_GROUP, GLA_GROUP)
        la = la_ref[pl.ds(base, GLA_GROUP), :]
        hi, mid, lo = _split3_bf16(la)
        cum = (jnp.dot(tri, hi, preferred_element_type=jnp.float32)
               + jnp.dot(tri, mid, preferred_element_type=jnp.float32)
               + jnp.dot(tri, lo, preferred_element_type=jnp.float32))
        for ci in range(chunks_per_group):
            rows = pl.ds(base + ci * CHUNK, CHUNK)
            cum_c = cum[ci * CHUNK:(ci + 1) * CHUNK]
            end = cum_c[CHUNK - 1:CHUNK]
            k_dec = (k_ref[rows, :].astype(jnp.float32) * jnp.exp(end - cum_c)).astype(jnp.bfloat16)
            u_t = lax.dot_general(v_ref[rows, :], k_dec, (((0,), (0,)), ((), ())),
                                  preferred_element_type=jnp.float32)
            st = jnp.exp(end) * st_ref[...] + u_t
            st_ref[...] = st
            o = lax.dot_general(q_ref[rows, :], st.astype(jnp.bfloat16), (((1,), (1,)), ((), ())),
                                preferred_element_type=jnp.float32) * (GLA_DK ** -0.5)
            ms = jnp.mean(o * o, axis=-1, keepdims=True)
            o = o * lax.rsqrt(ms + EPS) * g
            o_ref[rows, :] = (o * _silu(gate_ref[rows, :].astype(jnp.float32))).astype(o_ref.dtype)
        return carry

    lax.fori_loop(0, n_groups, group_step, 0)


def _gla(z, la, g_gla, batch, seq):
    m = z.shape[0]
    dk_blk = lambda off: off // GLA_DK
    dv_blk = lambda off: off // GLA_DV
    return pl.pallas_call(
        _gla_kernel,
        grid=(batch, GLA_HEADS),
        in_specs=[
            pl.BlockSpec((seq, GLA_DK), lambda b, h: (b, dk_blk(Z_GQ) + h)),
            pl.BlockSpec((seq, GLA_DK), lambda b, h: (b, dk_blk(Z_GK) + h)),
            pl.BlockSpec((seq, GLA_DV), lambda b, h: (b, dv_blk(Z_GV) + h)),
            pl.BlockSpec((seq, GLA_DV), lambda b, h: (b, dv_blk(Z_GG) + h)),
            pl.BlockSpec((seq, GLA_DK), lambda b, h: (b, h)),
            pl.BlockSpec((1, GLA_DV), lambda b, h: (0, h)),
        ],
        out_specs=pl.BlockSpec((seq, GLA_DV), lambda b, h: (b, h)),
        out_shape=jax.ShapeDtypeStruct((m, D_GLA), jnp.bfloat16),
        scratch_shapes=[pltpu.VMEM((GLA_DV, GLA_DK), jnp.float32)],
        compiler_params=pltpu.CompilerParams(
            dimension_semantics=("parallel", "parallel"), vmem_limit_bytes=VMEM_LIMIT),
        name="gla",
    )(z, z, z, z, la, g_gla)


def _attn_tile(q_ref, k_ref, v_ref, gate_ref, bias_ref, g, o_ref, q_start, k_start, width):
    q_rows = pl.ds(q_start, ATT_TQ)
    k_rows = pl.ds(k_start, width)
    s = lax.dot_general(q_ref[q_rows, :], k_ref[k_rows, :], (((1,), (1,)), ((), ())),
                        preferred_element_type=jnp.float32)
    s = s * (ATT_HD ** -0.5) + bias_ref[0, :, ATT_WIN - width:]
    m = jnp.max(s, axis=-1, keepdims=True)
    e = jnp.exp(s - m)
    denom = jnp.sum(e, axis=-1, keepdims=True)
    o = jnp.dot(e.astype(jnp.bfloat16), v_ref[k_rows, :], preferred_element_type=jnp.float32)
    o = o * (1.0 / denom)
    ms = jnp.mean(o * o, axis=-1, keepdims=True)
    o = o * lax.rsqrt(ms + EPS) * g
    o_ref[q_rows, :] = (o * _silu(gate_ref[q_rows, :].astype(jnp.float32))).astype(o_ref.dtype)


def _attn_kernel(q_ref, k_ref, v_ref, gate_ref, bias_ref, g_ref, o_ref):
    seq = q_ref.shape[0]
    n_tiles = seq // ATT_TQ
    n_head_tiles = (ATT_WIN - ATT_TQ) // ATT_TQ
    g = g_ref[...]
    tile = functools.partial(_attn_tile, q_ref, k_ref, v_ref, gate_ref, bias_ref, g, o_ref)
    for t in range(n_head_tiles):
        tile(t * ATT_TQ, 0, (t + 1) * ATT_TQ)

    def body(t, carry):
        q_start = pl.multiple_of(t * ATT_TQ, ATT_TQ)
        tile(q_start, pl.multiple_of(q_start - (ATT_WIN - ATT_TQ), ATT_TQ), ATT_WIN)
        return carry

    lax.fori_loop(n_head_tiles, n_tiles, body, 0)


def _band_attn(z, bias, g_att, batch, seq):
    m = z.shape[0]
    blk = lambda off: off // ATT_HD
    return pl.pallas_call(
        _attn_kernel,
        grid=(batch, ATT_HEADS),
        in_specs=[
            pl.BlockSpec((seq, ATT_HD), lambda b, h: (b, blk(Z_AQ) + h)),
            pl.BlockSpec((seq, ATT_HD), lambda b, h: (b, blk(Z_AK) + h)),
            pl.BlockSpec((seq, ATT_HD), lambda b, h: (b, blk(Z_AV) + h)),
            pl.BlockSpec((seq, ATT_HD), lambda b, h: (b, blk(Z_AG) + h)),
            pl.BlockSpec((1, ATT_TQ, ATT_WIN), lambda b, h: (h, 0, 0)),
            pl.BlockSpec((1, ATT_HD), lambda b, h: (0, h)),
        ],
        out_specs=pl.BlockSpec((seq, ATT_HD), lambda b, h: (b, h)),
        out_shape=jax.ShapeDtypeStruct((m, D_ATT), jnp.bfloat16),
        compiler_params=pltpu.CompilerParams(
            dimension_semantics=("parallel", "parallel"), vmem_limit_bytes=VMEM_LIMIT),
        name="band_attn",
    )(z, z, z, z, bias, g_att)


def _band_bias_table(rel_bias):
    i = np.arange(ATT_TQ)[:, None]
    j = np.arange(ATT_WIN)[None, :]
    rel = np.clip(LEFT_CHUNKS * CHUNK + i - j, -REL_CLIP, REL_CLIP) + REL_CLIP
    first = (i // CHUNK) * CHUNK
    in_band = (j >= first) & (j < first + (LEFT_CHUNKS + 1) * CHUNK)
    table = rel_bias.astype(jnp.float32)[:, rel]
    return jnp.where(in_band[None], table, -1e30)


def _out_proj_kernel(og_ref, oa_ref, wg_ref, wa_ref, x_ref, gpost_ref, y_ref):
    y = (jnp.dot(og_ref[...], wg_ref[...], preferred_element_type=jnp.float32)
         + jnp.dot(oa_ref[...], wa_ref[...], preferred_element_type=jnp.float32))
    ms = jnp.mean(y * y, axis=-1, keepdims=True)
    y_ref[...] = x_ref[...] + y * lax.rsqrt(ms + EPS) * gpost_ref[...]


def _out_proj(og, oa, w_g, w_a, x, g_post):
    m = x.shape[0]
    return pl.pallas_call(
        _out_proj_kernel,
        grid=(m // OUT_TM,),
        in_specs=[
            pl.BlockSpec((OUT_TM, D_GLA), lambda i: (i, 0)),
            pl.BlockSpec((OUT_TM, D_ATT), lambda i: (i, 0)),
            pl.BlockSpec((D_GLA, D_MODEL), lambda i: (0, 0)),
            pl.BlockSpec((D_ATT, D_MODEL), lambda i: (0, 0)),
            pl.BlockSpec((OUT_TM, D_MODEL), lambda i: (i, 0)),
            pl.BlockSpec((1, D_MODEL), lambda i: (0, 0)),
        ],
        out_specs=pl.BlockSpec((OUT_TM, D_MODEL), lambda i: (i, 0)),
        out_shape=jax.ShapeDtypeStruct((m, D_MODEL), jnp.float32),
        compiler_params=pltpu.CompilerParams(
            dimension_semantics=("parallel",), vmem_limit_bytes=VMEM_LIMIT),
        name="out_proj",
    )(og, oa, w_g, w_a, x, g_post)


def kernel(x, w_in, w_out, g_pre, g_post, w_alpha, b_alpha, g_gla, g_att, rel_bias):
    batch, seq, d_model = x.shape
    depth = w_in.shape[0]
    ga_lo = 2 * GLA_KW + 2 * D_GLA
    ga_hi = ga_lo + GATE_RANK
    h = x.reshape(batch * seq, d_model)
    for l in range(depth):
        w_main = jnp.concatenate([w_in[l, :, :ga_lo], w_in[l, :, ga_hi:]], axis=1).astype(jnp.bfloat16)
        w_ga = jnp.pad(w_in[l, :, ga_lo:ga_hi], ((0, 0), (0, GA_PAD - GATE_RANK))).astype(jnp.bfloat16)
        w_al = jnp.pad(w_alpha[l], ((0, GA_PAD - GATE_RANK), (0, 0))).astype(jnp.bfloat16)
        w_o = w_out[l].astype(jnp.bfloat16)
        bias = _band_bias_table(rel_bias[l])
        z, la = _in_proj(h, g_pre[l][None], w_main, w_ga, w_al, b_alpha[l][None])
        og = _gla(z, la, g_gla[l][None], batch, seq)
        oa = _band_attn(z, bias, g_att[l][None], batch, seq)
        h = _out_proj(og, oa, w_o[:D_GLA], w_o[D_GLA:], h, g_post[l][None])
    return h.reshape(batch, seq, d_model)
```

```python
import functools

import jax
import jax.numpy as jnp
import numpy as np
from jax import lax
from jax.experimental import pallas as pl
from jax.experimental.pallas import tpu as pltpu

D_MODEL = 2048
CHUNK = 64
D_GLA = 1024
D_ATT = 1024
GLA_HEADS = 4
GLA_DK = 128
GLA_DV = 256
GLA_KW = GLA_HEADS * GLA_DK
GATE_RANK = 16
GLA_TAU = 16.0
ATT_HEADS = 8
ATT_HD = 128
LEFT_CHUNKS = 8
REL_CLIP = 128
EPS = 1e-6

LANES = 128
GA_PAD = LANES
D_Z = 2 * GLA_KW + 2 * D_GLA + 4 * D_ATT

Z_GQ, Z_GK, Z_GV, Z_GG = 0, GLA_KW, 2 * GLA_KW, 2 * GLA_KW + D_GLA
Z_AQ = 2 * GLA_KW + 2 * D_GLA
Z_AK, Z_AV, Z_AG = Z_AQ + D_ATT, Z_AQ + 2 * D_ATT, Z_AQ + 3 * D_ATT

IN_TM, IN_TN = 1024, 512
OUT_TM = 256
GLA_GROUP = 256
ATT_TQ = 128
ATT_WIN = ATT_TQ + LEFT_CHUNKS * CHUNK
ATT_UNROLL = 4
VMEM_LIMIT = 48 * 1024 * 1024


def _silu(x):
    return x * (1.0 / (1.0 + jnp.exp(-x)))


def _log_sigmoid(x):
    return jnp.minimum(x, 0.0) - jnp.log1p(jnp.exp(-jnp.abs(x)))


def _in_proj_kernel(x_ref, gpre_ref, w_ref, wga_ref, walpha_ref, balpha_ref, z_ref, la_ref, h_ref):
    j = pl.program_id(1)

    @pl.when(j == 0)
    def _():
        x = x_ref[...]
        ms = jnp.mean(x * x, axis=-1, keepdims=True)
        h = (x * lax.rsqrt(ms + EPS) * gpre_ref[...]).astype(jnp.bfloat16)
        h_ref[...] = h
        ga = jnp.dot(h, wga_ref[...], preferred_element_type=jnp.float32)
        pre = jnp.dot(ga.astype(jnp.bfloat16), walpha_ref[...],
                      preferred_element_type=jnp.float32) + balpha_ref[...]
        la_ref[...] = _log_sigmoid(pre) * (1.0 / GLA_TAU)

    z_ref[...] = jnp.dot(h_ref[...], w_ref[...], preferred_element_type=jnp.float32).astype(z_ref.dtype)


def _in_proj(layer, x, g_pre, w_main, w_ga, w_alpha, b_alpha):
    m = x.shape[0]
    grid = (m // IN_TM, D_Z // IN_TN)
    return pl.pallas_call(
        _in_proj_kernel,
        grid=grid,
        in_specs=[
            pl.BlockSpec((IN_TM, D_MODEL), lambda i, j: (i, 0)),
            pl.BlockSpec((None, 1, D_MODEL), lambda i, j: (layer, 0, 0)),
            pl.BlockSpec((None, D_MODEL, IN_TN), lambda i, j: (layer, 0, j)),
            pl.BlockSpec((None, D_MODEL, GA_PAD), lambda i, j: (layer, 0, 0)),
            pl.BlockSpec((None, GA_PAD, GLA_KW), lambda i, j: (layer, 0, 0)),
            pl.BlockSpec((None, 1, GLA_KW), lambda i, j: (layer, 0, 0)),
        ],
        out_specs=[
            pl.BlockSpec((IN_TM, IN_TN), lambda i, j: (i, j)),
            pl.BlockSpec((IN_TM, GLA_KW), lambda i, j: (i, 0)),
        ],
        out_shape=[
            jax.ShapeDtypeStruct((m, D_Z), jnp.bfloat16),
            jax.ShapeDtypeStruct((m, GLA_KW), jnp.float32),
        ],
        scratch_shapes=[pltpu.VMEM((IN_TM, D_MODEL), jnp.bfloat16)],
        compiler_params=pltpu.CompilerParams(
            dimension_semantics=("parallel", "arbitrary"), vmem_limit_bytes=VMEM_LIMIT),
        name="in_proj",
    )(x, g_pre, w_main, w_ga, w_alpha, b_alpha)


def _split3_bf16(x):
    hi = x.astype(jnp.bfloat16)
    r1 = x - hi.astype(jnp.float32)
    mid = r1.astype(jnp.bfloat16)
    lo = (r1 - mid.astype(jnp.float32)).astype(jnp.bfloat16)
    return hi, mid, lo


def _gla_kernel(q_ref, k_ref, v_ref, gate_ref, la_ref, g_ref, o_ref, st_ref):
    seq = q_ref.shape[0]
    n_groups = seq // GLA_GROUP
    chunks_per_group = GLA_GROUP // CHUNK
    r = lax.broadcasted_iota(jnp.int32, (GLA_GROUP, GLA_GROUP), 0)
    c = lax.broadcasted_iota(jnp.int32, (GLA_GROUP, GLA_GROUP), 1)
    tri = jnp.where((r // CHUNK == c // CHUNK) & (c <= r), 1.0, 0.0).astype(jnp.bfloat16)
    st_ref[...] = jnp.zeros_like(st_ref)
    g = g_ref[...]

    def group_step(gi, carry):
        base = pl.multiple_of(gi * GLA_GROUP, GLA_GROUP)
        la = la_ref[pl.ds(base, GLA_GROUP), :]
        hi, mid, lo = _split3_bf16(la)
        cum = (jnp.dot(tri, hi, preferred_element_type=jnp.float32)
               + jnp.dot(tri, mid, preferred_element_type=jnp.float32)
               + jnp.dot(tri, lo, preferred_element_type=jnp.float32))
        for ci in range(chunks_per_group):
            rows = pl.ds(base + ci * CHUNK, CHUNK)
            cum_c = cum[ci * CHUNK:(ci + 1) * CHUNK]
            end = cum_c[CHUNK - 1:CHUNK]
            k_dec = (k_ref[rows, :].astype(jnp.float32) * jnp.exp(end - cum_c)).astype(jnp.bfloat16)
            u_t = lax.dot_general(v_ref[rows, :], k_dec, (((0,), (0,)), ((), ())),
                                  preferred_element_type=jnp.float32)
            st = jnp.exp(end) * st_ref[...] + u_t
            st_ref[...] = st
            o = lax.dot_general(q_ref[rows, :], st.astype(jnp.bfloat16), (((1,), (1,)), ((), ())),
                                preferred_element_type=jnp.float32) * (GLA_DK ** -0.5)
            ms = jnp.mean(o * o, axis=-1, keepdims=True)
            o = o * lax.rsqrt(ms + EPS) * g
            o_ref[rows, :] = (o * _silu(gate_ref[rows, :].astype(jnp.float32))).astype(o_ref.dtype)
        return carry

    lax.fori_loop(0, n_groups, group_step, 0)


def _gla(layer, z, la, g_gla, batch, seq):
    m = z.shape[0]
    dk_blk = lambda off: off // GLA_DK
    dv_blk = lambda off: off // GLA_DV
    return pl.pallas_call(
        _gla_kernel,
        grid=(batch, GLA_HEADS),
        in_specs=[
            pl.BlockSpec((seq, GLA_DK), lambda b, h: (b, dk_blk(Z_GQ) + h)),
            pl.BlockSpec((seq, GLA_DK), lambda b, h: (b, dk_blk(Z_GK) + h)),
            pl.BlockSpec((seq, GLA_DV), lambda b, h: (b, dv_blk(Z_GV) + h)),
            pl.BlockSpec((seq, GLA_DV), lambda b, h: (b, dv_blk(Z_GG) + h)),
            pl.BlockSpec((seq, GLA_DK), lambda b, h: (b, h)),
            pl.BlockSpec((None, 1, GLA_DV), lambda b, h: (layer, 0, h)),
        ],
        out_specs=pl.BlockSpec((seq, GLA_DV), lambda b, h: (b, h)),
        out_shape=jax.ShapeDtypeStruct((m, D_GLA), jnp.bfloat16),
        scratch_shapes=[pltpu.VMEM((GLA_DV, GLA_DK), jnp.float32)],
        compiler_params=pltpu.CompilerParams(
            dimension_semantics=("parallel", "parallel"), vmem_limit_bytes=VMEM_LIMIT),
        name="gla",
    )(z, z, z, z, la, g_gla)


def _attn_tile(q_ref, k_ref, v_ref, gate_ref, bias_ref, g, o_ref, q_start, k_start, width):
    q_rows = pl.ds(q_start, ATT_TQ)
    k_rows = pl.ds(k_start, width)
    s = lax.dot_general(q_ref[q_rows, :], k_ref[k_rows, :], (((1,), (1,)), ((), ())),
                        preferred_element_type=jnp.float32)
    s = s * (ATT_HD ** -0.5) + bias_ref[0, :, ATT_WIN - width:]
    m = jnp.max(s, axis=-1, keepdims=True)
    e = jnp.exp(s - m)
    denom = jnp.sum(e, axis=-1, keepdims=True)
    o = jnp.dot(e.astype(jnp.bfloat16), v_ref[k_rows, :], preferred_element_type=jnp.float32)
    o = o * (1.0 / denom)
    ms = jnp.mean(o * o, axis=-1, keepdims=True)
    o = o * lax.rsqrt(ms + EPS) * g
    o_ref[q_rows, :] = (o * _silu(gate_ref[q_rows, :].astype(jnp.float32))).astype(o_ref.dtype)


def _attn_kernel(q_ref, k_ref, v_ref, gate_ref, bias_ref, g_ref, o_ref):
    seq = q_ref.shape[0]
    n_tiles = seq // ATT_TQ
    n_head_tiles = (ATT_WIN - ATT_TQ) // ATT_TQ
    g = g_ref[...]
    tile = functools.partial(_attn_tile, q_ref, k_ref, v_ref, gate_ref, bias_ref, g, o_ref)
    for t in range(n_head_tiles):
        tile(t * ATT_TQ, 0, (t + 1) * ATT_TQ)

    def body(t, carry):
        q_start = pl.multiple_of(t * ATT_TQ, ATT_TQ)
        tile(q_start, pl.multiple_of(q_start - (ATT_WIN - ATT_TQ), ATT_TQ), ATT_WIN)
        return carry

    lax.fori_loop(n_head_tiles, n_tiles, body, 0, unroll=ATT_UNROLL)


def _band_attn(layer, z, bias, g_att, batch, seq):
    m = z.shape[0]
    blk = lambda off: off // ATT_HD
    return pl.pallas_call(
        _attn_kernel,
        grid=(batch, ATT_HEADS),
        in_specs=[
            pl.BlockSpec((seq, ATT_HD), lambda b, h: (b, blk(Z_AQ) + h)),
            pl.BlockSpec((seq, ATT_HD), lambda b, h: (b, blk(Z_AK) + h)),
            pl.BlockSpec((seq, ATT_HD), lambda b, h: (b, blk(Z_AV) + h)),
            pl.BlockSpec((seq, ATT_HD), lambda b, h: (b, blk(Z_AG) + h)),
            pl.BlockSpec((None, 1, ATT_TQ, ATT_WIN), lambda b, h: (layer, h, 0, 0)),
            pl.BlockSpec((None, 1, ATT_HD), lambda b, h: (layer, 0, h)),
        ],
        out_specs=pl.BlockSpec((seq, ATT_HD), lambda b, h: (b, h)),
        out_shape=jax.ShapeDtypeStruct((m, D_ATT), jnp.bfloat16),
        compiler_params=pltpu.CompilerParams(
            dimension_semantics=("parallel", "parallel"), vmem_limit_bytes=VMEM_LIMIT),
        name="band_attn",
    )(z, z, z, z, bias, g_att)


def _band_bias_table(rel_bias):
    lead = rel_bias.shape[:-1]
    span = ATT_TQ + ATT_WIN - 1
    n_far = ATT_WIN - REL_CLIP
    rb = rel_bias.astype(jnp.float32)
    c = jnp.concatenate([jnp.broadcast_to(rb[..., 2 * REL_CLIP:], lead + (n_far,)),
                         rb[..., 2 * REL_CLIP - 1:0:-1]], axis=-1)
    cp = jnp.pad(c, [(0, 0)] * len(lead) + [(0, 1)])
    skew = jnp.tile(cp, ATT_TQ)[..., :ATT_TQ * span].reshape(lead + (ATT_TQ, span))
    table = skew[..., ATT_TQ - 1:]
    i = np.arange(ATT_TQ)[:, None]
    j = np.arange(ATT_WIN)[None, :]
    first = (i // CHUNK) * CHUNK
    in_band = (j >= first) & (j < first + (LEFT_CHUNKS + 1) * CHUNK)
    return jnp.where(in_band, table, -1e30)


def _out_proj_kernel(og_ref, oa_ref, wg_ref, wa_ref, x_ref, gpost_ref, y_ref):
    y = (jnp.dot(og_ref[...], wg_ref[...], preferred_element_type=jnp.float32)
         + jnp.dot(oa_ref[...], wa_ref[...], preferred_element_type=jnp.float32))
    ms = jnp.mean(y * y, axis=-1, keepdims=True)
    y_ref[...] = x_ref[...] + y * lax.rsqrt(ms + EPS) * gpost_ref[...]


def _out_proj(layer, og, oa, w_out, x, g_post):
    m = x.shape[0]
    return pl.pallas_call(
        _out_proj_kernel,
        grid=(m // OUT_TM,),
        in_specs=[
            pl.BlockSpec((OUT_TM, D_GLA), lambda i: (i, 0)),
            pl.BlockSpec((OUT_TM, D_ATT), lambda i: (i, 0)),
            pl.BlockSpec((None, D_GLA, D_MODEL), lambda i: (layer, 0, 0)),
            pl.BlockSpec((None, D_ATT, D_MODEL), lambda i: (layer, 1, 0)),
            pl.BlockSpec((OUT_TM, D_MODEL), lambda i: (i, 0)),
            pl.BlockSpec((None, 1, D_MODEL), lambda i: (layer, 0, 0)),
        ],
        out_specs=pl.BlockSpec((OUT_TM, D_MODEL), lambda i: (i, 0)),
        out_shape=jax.ShapeDtypeStruct((m, D_MODEL), jnp.float32),
        compiler_params=pltpu.CompilerParams(
            dimension_semantics=("parallel",), vmem_limit_bytes=VMEM_LIMIT),
        name="out_proj",
    )(og, oa, w_out, w_out, x, g_post)


def kernel(x, w_in, w_out, g_pre, g_post, w_alpha, b_alpha, g_gla, g_att, rel_bias):
    batch, seq, d_model = x.shape
    depth = w_in.shape[0]
    ga_lo = 2 * GLA_KW + 2 * D_GLA
    ga_hi = ga_lo + GATE_RANK
    h = x.reshape(batch * seq, d_model)
    w_main = jnp.concatenate([w_in[:, :, :ga_lo], w_in[:, :, ga_hi:]], axis=2).astype(jnp.bfloat16)
    w_ga = jnp.pad(w_in[:, :, ga_lo:ga_hi], ((0, 0), (0, 0), (0, GA_PAD - GATE_RANK))).astype(jnp.bfloat16)
    w_al = jnp.pad(w_alpha, ((0, 0), (0, GA_PAD - GATE_RANK), (0, 0))).astype(jnp.bfloat16)
    w_o = w_out.astype(jnp.bfloat16)
    bias = _band_bias_table(rel_bias)
    row = lambda p: p[:, None, :]
    for l in range(depth):
        z, la = _in_proj(l, h, row(g_pre), w_main, w_ga, w_al, row(b_alpha))
        og = _gla(l, z, la, row(g_gla), batch, seq)
        oa = _band_attn(l, z, bias, row(g_att), batch, seq)
        h = _out_proj(l, og, oa, w_o, h, row(g_post))
    return h.reshape(batch, seq, d_model)
```

```python
import functools
import math

import jax
import jax.numpy as jnp
import numpy as np
from jax import lax
from jax.experimental import pallas as pl
from jax.experimental.pallas import tpu as pltpu

D_MODEL = 2048
CHUNK = 64
D_GLA = 1024
D_ATT = 1024
GLA_HEADS = 4
GLA_DK = 128
GLA_DV = 256
GLA_KW = GLA_HEADS * GLA_DK
GATE_RANK = 16
GLA_TAU = 16.0
ATT_HEADS = 8
ATT_HD = 128
LEFT_CHUNKS = 8
REL_CLIP = 128
EPS = 1e-6
LOG2E = math.log2(math.e)

LANES = 128

W_GA = 2 * GLA_KW + 2 * D_GLA
PROJ_TN = 512
Z_GQ, Z_GV, Z_GG = 0, GLA_KW, GLA_KW + D_GLA
Z_AQ = GLA_KW + 2 * D_GLA
Z_AV, Z_AG = Z_AQ + D_ATT, Z_AQ + 2 * D_ATT
D_Z = Z_AG + D_ATT
KT_G, KT_A = 0, GLA_KW
D_KT = GLA_KW + D_ATT

PROJ_TM = 2048
NORM_TM = 512
OUT_TM = 256
GLA_GROUP = 256
ATT_TQ = 128
ATT_WIN = ATT_TQ + LEFT_CHUNKS * CHUNK
CAST_ROWS = 256
VMEM_LIMIT = 48 * 1024 * 1024


def _silu(x):
    return x * (1.0 / (1.0 + jnp.exp(-x)))


def _log_sigmoid(x):
    return jnp.minimum(x, 0.0) - jnp.log1p(jnp.exp(-jnp.abs(x)))


def _rms_scale(x):
    return lax.rsqrt(jnp.mean(x * x, axis=-1, keepdims=True) + EPS)


def _norm_gate(h, gpre, wga, walt, bcol):
    hn = (h * _rms_scale(h) * gpre).astype(jnp.bfloat16)
    ga = jnp.dot(hn, wga.astype(jnp.bfloat16), preferred_element_type=jnp.float32)
    pre_t = lax.dot_general(walt, ga.astype(jnp.bfloat16), (((1,), (1,)), ((), ())),
                            preferred_element_type=jnp.float32) + bcol
    return hn, _log_sigmoid(pre_t) * (1.0 / GLA_TAU)


def _gate_specs(layer, grid_rank):
    z = (0,) * (grid_rank - 1)
    ix = lambda *blk: (lambda i, *rest: blk)
    return [
        pl.BlockSpec((None, 1, D_MODEL), ix(layer, 0, 0)),
        pl.BlockSpec((None, D_MODEL, LANES), ix(layer, 0, W_GA // LANES)),
        pl.BlockSpec((None, GLA_KW, LANES), ix(layer, 0, 0)),
        pl.BlockSpec((None, GLA_KW, 1), ix(layer, 0, 0)),
    ]


def _norm_gate_kernel(x_ref, gpre_ref, wga_ref, walt_ref, bcol_ref, hn_ref, lat_ref):
    hn, lat = _norm_gate(x_ref[...], gpre_ref[...], wga_ref[...], walt_ref[...], bcol_ref[...])
    hn_ref[...] = hn
    lat_ref[...] = lat


def _norm_gate_call(layer, x, g_pre, w_in, walt, bcol):
    m = x.shape[0]
    return pl.pallas_call(
        _norm_gate_kernel,
        grid=(m // NORM_TM,),
        in_specs=[pl.BlockSpec((NORM_TM, D_MODEL), lambda i: (i, 0))] + _gate_specs(layer, 1),
        out_specs=[
            pl.BlockSpec((NORM_TM, D_MODEL), lambda i: (i, 0)),
            pl.BlockSpec((GLA_KW, NORM_TM), lambda i: (0, i)),
        ],
        out_shape=[
            jax.ShapeDtypeStruct((m, D_MODEL), jnp.bfloat16),
            jax.ShapeDtypeStruct((GLA_KW, m), jnp.float32),
        ],
        compiler_params=pltpu.CompilerParams(
            dimension_semantics=("parallel",), vmem_limit_bytes=VMEM_LIMIT),
        name="norm_gate",
    )(x, g_pre, w_in, walt, bcol)


def _load_weight_block(w_ref, wx_ref, shifted, row0):
    rows = pl.ds(row0, CAST_ROWS)
    w = w_ref[rows, :]
    if not shifted:
        return w
    full = jnp.concatenate([w, wx_ref[rows, :]], axis=1)
    return pltpu.roll(full, PROJ_TN + LANES - GATE_RANK, axis=1)[:, :PROJ_TN]


def _proj_nn_kernel(n_plain, hn_ref, w_ref, wx_ref, z_ref, wbf_ref):
    j = pl.program_id(0)

    def relay(shifted):
        def body(r, carry):
            row0 = pl.multiple_of(r * CAST_ROWS, CAST_ROWS)
            wbf_ref[pl.ds(row0, CAST_ROWS), :] = _load_weight_block(w_ref, wx_ref, shifted, row0).astype(jnp.bfloat16)
            return carry
        lax.fori_loop(0, D_MODEL // CAST_ROWS, body, 0)

    @pl.when(pl.program_id(1) == 0)
    def _():
        pl.when(j < n_plain)(lambda: relay(False))
        pl.when(j >= n_plain)(lambda: relay(True))

    z_ref[...] = jnp.dot(hn_ref[...], wbf_ref[...], preferred_element_type=jnp.float32).astype(z_ref.dtype)


def _proj_nt_kernel(n_plain, hn_ref, w_ref, wx_ref, kt_ref, wt_ref):
    j = pl.program_id(0)

    def relay(shifted):
        def body(r, carry):
            row0 = pl.multiple_of(r * CAST_ROWS, CAST_ROWS)
            w = _load_weight_block(w_ref, wx_ref, shifted, row0)
            wt_ref[:, pl.ds(row0, CAST_ROWS)] = w.T.astype(jnp.bfloat16)
            return carry
        lax.fori_loop(0, D_MODEL // CAST_ROWS, body, 0)

    @pl.when(pl.program_id(1) == 0)
    def _():
        pl.when(j < n_plain)(lambda: relay(False))
        pl.when(j >= n_plain)(lambda: relay(True))

    kt_ref[...] = lax.dot_general(wt_ref[...], hn_ref[...], (((1,), (1,)), ((), ())),
                                  preferred_element_type=jnp.float32).astype(kt_ref.dtype)


_W_BLK = lambda col: col // PROJ_TN
_NN_SRC = ([_W_BLK(0)] + [_W_BLK(2 * GLA_KW) + t for t in range(2 * D_GLA // PROJ_TN)]
           + [_W_BLK(W_GA) + t for t in range(D_ATT // PROJ_TN)]
           + [_W_BLK(W_GA + 2 * D_ATT) + t for t in range(2 * D_ATT // PROJ_TN)])
_NN_PLAIN = 1 + 2 * D_GLA // PROJ_TN
_NT_SRC = [_W_BLK(GLA_KW)] + [_W_BLK(W_GA + D_ATT) + t for t in range(D_ATT // PROJ_TN)]
_NT_PLAIN = 1


def _src_block(table, j):
    blk = table[0] + j * 0
    for t in range(1, len(table)):
        blk = jnp.where(j >= t, table[t], blk)
    return blk


def _proj(layer, hn, w_in, transposed):
    m = hn.shape[0]
    table, n_plain = (_NT_SRC, _NT_PLAIN) if transposed else (_NN_SRC, _NN_PLAIN)
    grid = (len(table), m // PROJ_TM)
    lanes_per_blk = PROJ_TN // LANES
    in_specs = [
        pl.BlockSpec((PROJ_TM, D_MODEL), lambda j, i: (i, 0)),
        pl.BlockSpec((None, D_MODEL, PROJ_TN), lambda j, i: (layer, 0, _src_block(table, j))),
        pl.BlockSpec((None, D_MODEL, LANES), lambda j, i: (layer, 0, (_src_block(table, j) + 1) * lanes_per_blk)),
    ]
    if transposed:
        kern = functools.partial(_proj_nt_kernel, n_plain)
        out_spec = pl.BlockSpec((PROJ_TN, PROJ_TM), lambda j, i: (j, i))
        out_shape = jax.ShapeDtypeStruct((D_KT, m), jnp.bfloat16)
        scratch = pltpu.VMEM((PROJ_TN, D_MODEL), jnp.bfloat16)
    else:
        kern = functools.partial(_proj_nn_kernel, n_plain)
        out_spec = pl.BlockSpec((PROJ_TM, PROJ_TN), lambda j, i: (i, j))
        out_shape = jax.ShapeDtypeStruct((m, D_Z), jnp.bfloat16)
        scratch = pltpu.VMEM((D_MODEL, PROJ_TN), jnp.bfloat16)
    return pl.pallas_call(
        kern,
        grid=grid,
        in_specs=in_specs,
        out_specs=out_spec,
        out_shape=out_shape,
        scratch_shapes=[scratch],
        compiler_params=pltpu.CompilerParams(
            dimension_semantics=("arbitrary", "arbitrary"), vmem_limit_bytes=VMEM_LIMIT),
        name="proj_nt" if transposed else "proj_nn",
    )(hn, w_in, w_in)


def _split3_bf16(x):
    hi = x.astype(jnp.bfloat16)
    r1 = x - hi.astype(jnp.float32)
    mid = r1.astype(jnp.bfloat16)
    lo = (r1 - mid.astype(jnp.float32)).astype(jnp.bfloat16)
    return hi, mid, lo


def _gla_kernel(q_ref, kt_ref, v_ref, gate_ref, lat_ref, g_ref, o_ref, st_ref):
    seq = q_ref.shape[0]
    n_groups = seq // GLA_GROUP
    chunks_per_group = GLA_GROUP // CHUNK
    r = lax.broadcasted_iota(jnp.int32, (GLA_GROUP, GLA_GROUP), 0)
    c = lax.broadcasted_iota(jnp.int32, (GLA_GROUP, GLA_GROUP), 1)
    triu = jnp.where(r <= c, 1.0, 0.0).astype(jnp.bfloat16)
    frame = lax.broadcasted_iota(jnp.int32, (GLA_DK, GLA_GROUP), 1)
    st_ref[...] = jnp.zeros_like(st_ref)
    g = g_ref[...]

    def group_step(gi, carry):
        base = pl.multiple_of(gi * GLA_GROUP, GLA_GROUP)
        cols = pl.ds(base, GLA_GROUP)
        parts = jnp.concatenate(_split3_bf16(lat_ref[:, cols]), axis=0)
        cum3 = jnp.dot(parts, triu, preferred_element_type=jnp.float32)
        cum = cum3[:GLA_DK] + cum3[GLA_DK:2 * GLA_DK] + cum3[2 * GLA_DK:]
        kt = kt_ref[:, cols].astype(jnp.float32)
        v_g = v_ref[cols, :]
        s0 = st_ref[...]
        s_c = s0
        for ci in range(chunks_per_group):
            last = (ci + 1) * CHUNK - 1
            end_b = jnp.broadcast_to(cum[:, last:last + 1], (GLA_DK, GLA_GROUP))
            arg = jnp.where(frame <= last, end_b - cum, -1e30)
            k_dec = (kt * jnp.exp(arg)).astype(jnp.bfloat16)
            s_c = jnp.exp(end_b) * s0 + jnp.dot(k_dec, v_g, preferred_element_type=jnp.float32)
            rows = pl.ds(base + ci * CHUNK, CHUNK)
            o = jnp.dot(q_ref[rows, :], s_c.astype(jnp.bfloat16),
                        preferred_element_type=jnp.float32) * (GLA_DK ** -0.5)
            o = o * _rms_scale(o) * g
            o_ref[rows, :] = (o * _silu(gate_ref[rows, :].astype(jnp.float32))).astype(o_ref.dtype)
        st_ref[...] = s_c
        return carry

    lax.fori_loop(0, n_groups, group_step, 0)


def _gla(layer, z, kt, lat, g_gla, batch, seq):
    assert GLA_GROUP == GLA_DV
    m = z.shape[0]
    return pl.pallas_call(
        _gla_kernel,
        grid=(batch, GLA_HEADS),
        in_specs=[
            pl.BlockSpec((seq, GLA_DK), lambda b, h: (b, Z_GQ // GLA_DK + h)),
            pl.BlockSpec((GLA_DK, seq), lambda b, h: (KT_G // GLA_DK + h, b)),
            pl.BlockSpec((seq, GLA_DV), lambda b, h: (b, Z_GV // GLA_DV + h)),
            pl.BlockSpec((seq, GLA_DV), lambda b, h: (b, Z_GG // GLA_DV + h)),
            pl.BlockSpec((GLA_DK, seq), lambda b, h: (h, b)),
            pl.BlockSpec((None, 1, GLA_DV), lambda b, h: (layer, 0, h)),
        ],
        out_specs=pl.BlockSpec((seq, GLA_DV), lambda b, h: (b, h)),
        out_shape=jax.ShapeDtypeStruct((m, D_GLA), jnp.bfloat16),
        scratch_shapes=[pltpu.VMEM((GLA_DK, GLA_DV), jnp.float32)],
        compiler_params=pltpu.CompilerParams(
            dimension_semantics=("parallel", "parallel"), vmem_limit_bytes=VMEM_LIMIT),
        name="gla",
    )(z, kt, z, z, lat, g_gla)


def _attn_scores(q_ref, kt_ref, bias_ref, q_start, k_start, width):
    s = jnp.dot(q_ref[q_start:q_start + ATT_TQ, :], kt_ref[:, k_start:k_start + width],
                preferred_element_type=jnp.float32)
    return s * (ATT_HD ** -0.5 * LOG2E) + bias_ref[0, :, ATT_WIN - width:]


def _attn_finish(s2, v_ref, gate_ref, g, o_ref, q_start, k_start, width):
    m = jnp.max(s2, axis=-1, keepdims=True)
    e = jnp.exp2(s2 - m)
    denom = jnp.sum(e, axis=-1, keepdims=True)
    o = jnp.dot(e.astype(jnp.bfloat16), v_ref[k_start:k_start + width, :], preferred_element_type=jnp.float32)
    o = o * (1.0 / denom)
    o = o * _rms_scale(o) * g
    rows = slice(q_start, q_start + ATT_TQ)
    o_ref[rows, :] = (o * _silu(gate_ref[rows, :].astype(jnp.float32))).astype(o_ref.dtype)


def _attn_kernel(q_ref, kt_ref, v_ref, gate_ref, bias_ref, g_ref, o_ref):
    seq = q_ref.shape[0]
    g = g_ref[...]

    def window(t):
        k_start = max(0, (t + 1) * ATT_TQ - ATT_WIN)
        return t * ATT_TQ, k_start, (t + 1) * ATT_TQ - k_start

    n_tiles = seq // ATT_TQ
    s_next = _attn_scores(q_ref, kt_ref, bias_ref, *window(0))
    for t in range(n_tiles):
        s_cur = s_next
        if t + 1 < n_tiles:
            s_next = _attn_scores(q_ref, kt_ref, bias_ref, *window(t + 1))
        _attn_finish(s_cur, v_ref, gate_ref, g, o_ref, *window(t))


def _band_attn(layer, z, kt, bias, g_att, batch, seq):
    m = z.shape[0]
    blk = lambda off: off // ATT_HD
    return pl.pallas_call(
        _attn_kernel,
        grid=(batch, ATT_HEADS),
        in_specs=[
            pl.BlockSpec((seq, ATT_HD), lambda b, h: (b, blk(Z_AQ) + h)),
            pl.BlockSpec((ATT_HD, seq), lambda b, h: (blk(KT_A) + h, b)),
            pl.BlockSpec((seq, ATT_HD), lambda b, h: (b, blk(Z_AV) + h)),
            pl.BlockSpec((seq, ATT_HD), lambda b, h: (b, blk(Z_AG) + h)),
            pl.BlockSpec((None, 1, ATT_TQ, ATT_WIN), lambda b, h: (layer, h, 0, 0)),
            pl.BlockSpec((None, 1, ATT_HD), lambda b, h: (layer, 0, h)),
        ],
        out_specs=pl.BlockSpec((seq, ATT_HD), lambda b, h: (b, h)),
        out_shape=jax.ShapeDtypeStruct((m, D_ATT), jnp.bfloat16),
        compiler_params=pltpu.CompilerParams(
            dimension_semantics=("parallel", "parallel"), vmem_limit_bytes=VMEM_LIMIT),
        name="band_attn",
    )(z, kt, z, z, bias, g_att)


def _band_bias_table(rel_bias):
    lead = rel_bias.shape[:-1]
    span = ATT_TQ + ATT_WIN - 1
    n_far = ATT_WIN - REL_CLIP
    rb = rel_bias.astype(jnp.float32) * LOG2E
    c = jnp.concatenate([jnp.broadcast_to(rb[..., 2 * REL_CLIP:], lead + (n_far,)),
                         rb[..., 2 * REL_CLIP - 1:0:-1]], axis=-1)
    cp = jnp.pad(c, [(0, 0)] * len(lead) + [(0, 1)])
    skew = jnp.tile(cp, ATT_TQ)[..., :ATT_TQ * span].reshape(lead + (ATT_TQ, span))
    table = skew[..., ATT_TQ - 1:]
    i = np.arange(ATT_TQ)[:, None]
    j = np.arange(ATT_WIN)[None, :]
    first = (i // CHUNK) * CHUNK
    in_band = (j >= first) & (j < first + (LEFT_CHUNKS + 1) * CHUNK)
    return jnp.where(in_band, table, -1e30)


def _out_proj_kernel(fused, og_ref, oa_ref, wg_ref, wa_ref, x_ref, gpost_ref, *rest):
    y = (jnp.dot(og_ref[...], wg_ref[...], preferred_element_type=jnp.float32)
         + jnp.dot(oa_ref[...], wa_ref[...], preferred_element_type=jnp.float32))
    h = x_ref[...] + y * _rms_scale(y) * gpost_ref[...]
    if fused:
        gpre_ref, wga_ref, walt_ref, bcol_ref, h_ref, hn_ref, lat_ref = rest
        hn, lat = _norm_gate(h, gpre_ref[...], wga_ref[...], walt_ref[...], bcol_ref[...])
        hn_ref[...] = hn
        lat_ref[...] = lat
    else:
        (h_ref,) = rest
    h_ref[...] = h


def _out_proj(layer, og, oa, w_out, x, g_post, gate_args=None):
    m = x.shape[0]
    fused = gate_args is not None
    in_specs = [
        pl.BlockSpec((OUT_TM, D_GLA), lambda i: (i, 0)),
        pl.BlockSpec((OUT_TM, D_ATT), lambda i: (i, 0)),
        pl.BlockSpec((None, D_GLA, D_MODEL), lambda i: (layer, 0, 0)),
        pl.BlockSpec((None, D_ATT, D_MODEL), lambda i: (layer, 1, 0)),
        pl.BlockSpec((OUT_TM, D_MODEL), lambda i: (i, 0)),
        pl.BlockSpec((None, 1, D_MODEL), lambda i: (layer, 0, 0)),
    ]
    out_specs = [pl.BlockSpec((OUT_TM, D_MODEL), lambda i: (i, 0))]
    out_shape = [jax.ShapeDtypeStruct((m, D_MODEL), jnp.float32)]
    args = [og, oa, w_out, w_out, x, g_post]
    if fused:
        in_specs += _gate_specs(layer + 1, 1)
        out_specs += [pl.BlockSpec((OUT_TM, D_MODEL), lambda i: (i, 0)),
                      pl.BlockSpec((GLA_KW, OUT_TM), lambda i: (0, i))]
        out_shape += [jax.ShapeDtypeStruct((m, D_MODEL), jnp.bfloat16),
                      jax.ShapeDtypeStruct((GLA_KW, m), jnp.float32)]
        args += list(gate_args)
    return pl.pallas_call(
        functools.partial(_out_proj_kernel, fused),
        grid=(m // OUT_TM,),
        in_specs=in_specs,
        out_specs=out_specs,
        out_shape=out_shape,
        compiler_params=pltpu.CompilerParams(
            dimension_semantics=("parallel",), vmem_limit_bytes=VMEM_LIMIT),
        name="out_proj",
    )(*args)


def kernel(x, w_in, w_out, g_pre, g_post, w_alpha, b_alpha, g_gla, g_att, rel_bias):
    batch, seq, d_model = x.shape
    depth = w_in.shape[0]
    h = x.reshape(batch * seq, d_model)
    walt = jnp.pad(jnp.swapaxes(w_alpha, 1, 2), ((0, 0), (0, 0), (0, LANES - GATE_RANK))).astype(jnp.bfloat16)
    bcol = b_alpha[:, :, None]
    w_o = w_out.astype(jnp.bfloat16)
    bias = _band_bias_table(rel_bias)
    row = lambda p: p[:, None, :]
    gate_args = (row(g_pre), w_in, walt, bcol)
    hn, lat = _norm_gate_call(0, h, *gate_args)
    for l in range(depth):
        z = _proj(l, hn, w_in, transposed=False)
        kt = _proj(l, hn, w_in, transposed=True)
        og = _gla(l, z, kt, lat, row(g_gla), batch, seq)
        oa = _band_attn(l, z, kt, bias, row(g_att), batch, seq)
        if l + 1 < depth:
            h, hn, lat = _out_proj(l, og, oa, w_o, h, row(g_post), gate_args)
        else:
            (h,) = _out_proj(l, og, oa, w_o, h, row(g_post))
    return h.reshape(batch, seq, d_model)
```

```python
import functools
import math

import jax
import jax.numpy as jnp
import numpy as np
from jax import lax
from jax.experimental import pallas as pl
from jax.experimental.pallas import tpu as pltpu

D_MODEL = 2048
CHUNK = 64
D_GLA = 1024
D_ATT = 1024
GLA_HEADS = 4
GLA_DK = 128
GLA_DV = 256
GLA_KW = GLA_HEADS * GLA_DK
GATE_RANK = 16
GLA_TAU = 16.0
ATT_HEADS = 8
ATT_HD = 128
LEFT_CHUNKS = 8
REL_CLIP = 128
EPS = 1e-6
LOG2E = math.log2(math.e)

LANES = 128

W_GA = 2 * GLA_KW + 2 * D_GLA
W_AQ = W_GA + GATE_RANK
D_IN = W_AQ + 4 * D_ATT
ROW_ALIGN = GATE_RANK
PROJ_TN = 512
Z_GQ, Z_GV, Z_GG = 0, GLA_KW, GLA_KW + D_GLA
Z_AQ = GLA_KW + 2 * D_GLA
Z_AV, Z_AG = Z_AQ + D_ATT, Z_AQ + 2 * D_ATT
D_Z = Z_AG + D_ATT
KT_G, KT_A = 0, GLA_KW
D_KT = GLA_KW + D_ATT

PROJ_TM = 2048
NORM_TM = 512
OUT_TM = 256
GLA_GROUP = 256
ATT_TQ = 128
ATT_WIN = ATT_TQ + LEFT_CHUNKS * CHUNK
CAST_ROWS = 256
VMEM_LIMIT = 48 * 1024 * 1024


def _silu(x):
    return x * (1.0 / (1.0 + jnp.exp(-x)))


def _log_sigmoid(x):
    return jnp.minimum(x, 0.0) - jnp.log1p(jnp.exp(-jnp.abs(x)))


def _rms_scale(x):
    return lax.rsqrt(jnp.mean(x * x, axis=-1, keepdims=True) + EPS)


def _norm_gate(h, gpre, wga, walt, bcol):
    hn = (h * _rms_scale(h) * gpre).astype(jnp.bfloat16)
    ga = lax.dot_general(hn, wga.astype(jnp.bfloat16), (((1,), (1,)), ((), ())),
                         preferred_element_type=jnp.float32)
    pre_t = lax.dot_general(walt, ga.astype(jnp.bfloat16), (((1,), (1,)), ((), ())),
                            preferred_element_type=jnp.float32) + bcol
    return hn, _log_sigmoid(pre_t) * (1.0 / GLA_TAU)


def _gate_specs(layer, grid_rank):
    ix = lambda *blk: (lambda i, *rest: blk)
    return [
        pl.BlockSpec((None, 1, D_MODEL), ix(layer, 0, 0)),
        pl.BlockSpec((pl.Element(LANES), pl.Element(D_MODEL)), ix(layer * D_IN + W_GA, 0)),
        pl.BlockSpec((None, GLA_KW, LANES), ix(layer, 0, 0)),
        pl.BlockSpec((None, GLA_KW, 1), ix(layer, 0, 0)),
    ]


def _norm_gate_kernel(x_ref, gpre_ref, wga_ref, walt_ref, bcol_ref, hn_ref, lat_ref):
    hn, lat = _norm_gate(x_ref[...], gpre_ref[...], wga_ref[...], walt_ref[...], bcol_ref[...])
    hn_ref[...] = hn
    lat_ref[...] = lat


def _norm_gate_call(layer, x, g_pre, w_in, walt, bcol):
    m = x.shape[0]
    return pl.pallas_call(
        _norm_gate_kernel,
        grid=(m // NORM_TM,),
        in_specs=[pl.BlockSpec((NORM_TM, D_MODEL), lambda i: (i, 0))] + _gate_specs(layer, 1),
        out_specs=[
            pl.BlockSpec((NORM_TM, D_MODEL), lambda i: (i, 0)),
            pl.BlockSpec((GLA_KW, NORM_TM), lambda i: (0, i)),
        ],
        out_shape=[
            jax.ShapeDtypeStruct((m, D_MODEL), jnp.bfloat16),
            jax.ShapeDtypeStruct((GLA_KW, m), jnp.float32),
        ],
        compiler_params=pltpu.CompilerParams(
            dimension_semantics=("parallel",), vmem_limit_bytes=VMEM_LIMIT),
        name="norm_gate",
    )(x, g_pre, w_in, walt, bcol)


def _proj_kernel(transposed, hn_ref, wt_ref, out_ref, wbf_ref):
    @pl.when(pl.program_id(1) == 0)
    def _():
        def body(r, carry):
            rows = pl.ds(pl.multiple_of(r * CAST_ROWS, CAST_ROWS), CAST_ROWS)
            wbf_ref[rows, :] = wt_ref[rows, :].astype(jnp.bfloat16)
            return carry
        lax.fori_loop(0, PROJ_TN // CAST_ROWS, body, 0)

    lhs, rhs = (wbf_ref, hn_ref) if transposed else (hn_ref, wbf_ref)
    out_ref[...] = lax.dot_general(lhs[...], rhs[...], (((1,), (1,)), ((), ())),
                                   preferred_element_type=jnp.float32).astype(out_ref.dtype)


_NN_ROWS = ([0] + [2 * GLA_KW + t * PROJ_TN for t in range(2 * D_GLA // PROJ_TN)]
            + [W_AQ + t * PROJ_TN for t in range(D_ATT // PROJ_TN)]
            + [W_AQ + 2 * D_ATT + t * PROJ_TN for t in range(2 * D_ATT // PROJ_TN)])
_NT_ROWS = [GLA_KW] + [W_AQ + D_ATT + t * PROJ_TN for t in range(D_ATT // PROJ_TN)]


def _table_lookup(table, j):
    val = table[0] + j * 0
    for t in range(1, len(table)):
        val = jnp.where(j >= t, table[t], val)
    return val


def _proj(layer, hn, w_in_t, transposed):
    m = hn.shape[0]
    table = _NT_ROWS if transposed else _NN_ROWS
    grid = (len(table), m // PROJ_TM)
    in_specs = [
        pl.BlockSpec((PROJ_TM, D_MODEL), lambda j, i: (i, 0)),
        pl.BlockSpec((pl.Element(PROJ_TN), pl.Element(D_MODEL)),
                     lambda j, i: (_table_lookup([(layer * D_IN + r) // ROW_ALIGN for r in table], j) * ROW_ALIGN, 0)),
    ]
    if transposed:
        out_spec = pl.BlockSpec((PROJ_TN, PROJ_TM), lambda j, i: (j, i))
        out_shape = jax.ShapeDtypeStruct((D_KT, m), jnp.bfloat16)
    else:
        out_spec = pl.BlockSpec((PROJ_TM, PROJ_TN), lambda j, i: (i, j))
        out_shape = jax.ShapeDtypeStruct((m, D_Z), jnp.bfloat16)
    return pl.pallas_call(
        functools.partial(_proj_kernel, transposed),
        grid=grid,
        in_specs=in_specs,
        out_specs=out_spec,
        out_shape=out_shape,
        scratch_shapes=[pltpu.VMEM((PROJ_TN, D_MODEL), jnp.bfloat16)],
        compiler_params=pltpu.CompilerParams(
            dimension_semantics=("arbitrary", "arbitrary"), vmem_limit_bytes=VMEM_LIMIT),
        name="proj_nt" if transposed else "proj_nn",
    )(hn, w_in_t)


def _split3_bf16(x):
    hi = x.astype(jnp.bfloat16)
    r1 = x - hi.astype(jnp.float32)
    mid = r1.astype(jnp.bfloat16)
    lo = (r1 - mid.astype(jnp.float32)).astype(jnp.bfloat16)
    return hi, mid, lo


def _gla_kernel(q_ref, kt_ref, v_ref, gate_ref, lat_ref, g_ref, o_ref, st_ref):
    seq = q_ref.shape[0]
    n_groups = seq // GLA_GROUP
    chunks_per_group = GLA_GROUP // CHUNK
    r = lax.broadcasted_iota(jnp.int32, (GLA_GROUP, GLA_GROUP), 0)
    c = lax.broadcasted_iota(jnp.int32, (GLA_GROUP, GLA_GROUP), 1)
    triu = jnp.where(r <= c, 1.0, 0.0).astype(jnp.bfloat16)
    frame = lax.broadcasted_iota(jnp.int32, (GLA_DK, GLA_GROUP), 1)
    st_ref[...] = jnp.zeros_like(st_ref)
    g = g_ref[...]

    def group_step(gi, carry):
        base = pl.multiple_of(gi * GLA_GROUP, GLA_GROUP)
        cols = pl.ds(base, GLA_GROUP)
        parts = jnp.concatenate(_split3_bf16(lat_ref[:, cols]), axis=0)
        cum3 = jnp.dot(parts, triu, preferred_element_type=jnp.float32)
        cum = cum3[:GLA_DK] + cum3[GLA_DK:2 * GLA_DK] + cum3[2 * GLA_DK:]
        kt = kt_ref[:, cols].astype(jnp.float32)
        v_g = v_ref[cols, :]
        s0 = st_ref[...]
        s_c = s0
        for ci in range(chunks_per_group):
            last = (ci + 1) * CHUNK - 1
            end_b = jnp.broadcast_to(cum[:, last:last + 1], (GLA_DK, GLA_GROUP))
            arg = jnp.where(frame <= last, end_b - cum, -1e30)
            k_dec = (kt * jnp.exp(arg)).astype(jnp.bfloat16)
            s_c = jnp.exp(end_b) * s0 + jnp.dot(k_dec, v_g, preferred_element_type=jnp.float32)
            rows = pl.ds(base + ci * CHUNK, CHUNK)
            o = jnp.dot(q_ref[rows, :], s_c.astype(jnp.bfloat16),
                        preferred_element_type=jnp.float32) * (GLA_DK ** -0.5)
            o = o * _rms_scale(o) * g
            o_ref[rows, :] = (o * _silu(gate_ref[rows, :].astype(jnp.float32))).astype(o_ref.dtype)
        st_ref[...] = s_c
        return carry

    lax.fori_loop(0, n_groups, group_step, 0)


def _gla(layer, z, kt, lat, g_gla, batch, seq):
    assert GLA_GROUP == GLA_DV
    m = z.shape[0]
    return pl.pallas_call(
        _gla_kernel,
        grid=(batch, GLA_HEADS),
        in_specs=[
            pl.BlockSpec((seq, GLA_DK), lambda b, h: (b, Z_GQ // GLA_DK + h)),
            pl.BlockSpec((GLA_DK, seq), lambda b, h: (KT_G // GLA_DK + h, b)),
            pl.BlockSpec((seq, GLA_DV), lambda b, h: (b, Z_GV // GLA_DV + h)),
            pl.BlockSpec((seq, GLA_DV), lambda b, h: (b, Z_GG // GLA_DV + h)),
            pl.BlockSpec((GLA_DK, seq), lambda b, h: (h, b)),
            pl.BlockSpec((None, 1, GLA_DV), lambda b, h: (layer, 0, h)),
        ],
        out_specs=pl.BlockSpec((seq, GLA_DV), lambda b, h: (b, h)),
        out_shape=jax.ShapeDtypeStruct((m, D_GLA), jnp.bfloat16),
        scratch_shapes=[pltpu.VMEM((GLA_DK, GLA_DV), jnp.float32)],
        compiler_params=pltpu.CompilerParams(
            dimension_semantics=("parallel", "parallel"), vmem_limit_bytes=VMEM_LIMIT),
        name="gla",
    )(z, kt, z, z, lat, g_gla)


def _attn_scores(q_ref, kt_ref, bias_ref, q_start, k_start, width):
    s = jnp.dot(q_ref[q_start:q_start + ATT_TQ, :], kt_ref[:, k_start:k_start + width],
                preferred_element_type=jnp.float32)
    return s * (ATT_HD ** -0.5 * LOG2E) + bias_ref[0, :, ATT_WIN - width:]


def _attn_finish(s2, v_ref, gate_ref, g, o_ref, q_start, k_start, width):
    m = jnp.max(s2, axis=-1, keepdims=True)
    e = jnp.exp2(s2 - m)
    denom = jnp.sum(e, axis=-1, keepdims=True)
    o = jnp.dot(e.astype(jnp.bfloat16), v_ref[k_start:k_start + width, :], preferred_element_type=jnp.float32)
    o = o * (1.0 / denom)
    o = o * _rms_scale(o) * g
    rows = slice(q_start, q_start + ATT_TQ)
    o_ref[rows, :] = (o * _silu(gate_ref[rows, :].astype(jnp.float32))).astype(o_ref.dtype)


def _attn_kernel(q_ref, kt_ref, v_ref, gate_ref, bias_ref, g_ref, o_ref):
    seq = q_ref.shape[0]
    g = g_ref[...]

    def window(t):
        k_start = max(0, (t + 1) * ATT_TQ - ATT_WIN)
        return t * ATT_TQ, k_start, (t + 1) * ATT_TQ - k_start

    n_tiles = seq // ATT_TQ
    s_next = _attn_scores(q_ref, kt_ref, bias_ref, *window(0))
    for t in range(n_tiles):
        s_cur = s_next
        if t + 1 < n_tiles:
            s_next = _attn_scores(q_ref, kt_ref, bias_ref, *window(t + 1))
        _attn_finish(s_cur, v_ref, gate_ref, g, o_ref, *window(t))


def _band_attn(layer, z, kt, bias, g_att, batch, seq):
    m = z.shape[0]
    blk = lambda off: off // ATT_HD
    return pl.pallas_call(
        _attn_kernel,
        grid=(batch, ATT_HEADS),
        in_specs=[
            pl.BlockSpec((seq, ATT_HD), lambda b, h: (b, blk(Z_AQ) + h)),
            pl.BlockSpec((ATT_HD, seq), lambda b, h: (blk(KT_A) + h, b)),
            pl.BlockSpec((seq, ATT_HD), lambda b, h: (b, blk(Z_AV) + h)),
            pl.BlockSpec((seq, ATT_HD), lambda b, h: (b, blk(Z_AG) + h)),
            pl.BlockSpec((None, 1, ATT_TQ, ATT_WIN), lambda b, h: (layer, h, 0, 0)),
            pl.BlockSpec((None, 1, ATT_HD), lambda b, h: (layer, 0, h)),
        ],
        out_specs=pl.BlockSpec((seq, ATT_HD), lambda b, h: (b, h)),
        out_shape=jax.ShapeDtypeStruct((m, D_ATT), jnp.bfloat16),
        compiler_params=pltpu.CompilerParams(
            dimension_semantics=("parallel", "parallel"), vmem_limit_bytes=VMEM_LIMIT),
        name="band_attn",
    )(z, kt, z, z, bias, g_att)


def _band_bias_table(rel_bias):
    lead = rel_bias.shape[:-1]
    span = ATT_TQ + ATT_WIN - 1
    n_far = ATT_WIN - REL_CLIP
    rb = rel_bias.astype(jnp.float32) * LOG2E
    c = jnp.concatenate([jnp.broadcast_to(rb[..., 2 * REL_CLIP:], lead + (n_far,)),
                         rb[..., 2 * REL_CLIP - 1:0:-1]], axis=-1)
    cp = jnp.pad(c, [(0, 0)] * len(lead) + [(0, 1)])
    skew = jnp.tile(cp, ATT_TQ)[..., :ATT_TQ * span].reshape(lead + (ATT_TQ, span))
    table = skew[..., ATT_TQ - 1:]
    i = np.arange(ATT_TQ)[:, None]
    j = np.arange(ATT_WIN)[None, :]
    first = (i // CHUNK) * CHUNK
    in_band = (j >= first) & (j < first + (LEFT_CHUNKS + 1) * CHUNK)
    return jnp.where(in_band, table, -1e30)


def _out_proj_kernel(fused, og_ref, oa_ref, wg_ref, wa_ref, x_ref, gpost_ref, *rest):
    y = (jnp.dot(og_ref[...], wg_ref[...], preferred_element_type=jnp.float32)
         + jnp.dot(oa_ref[...], wa_ref[...], preferred_element_type=jnp.float32))
    h = x_ref[...] + y * _rms_scale(y) * gpost_ref[...]
    if fused:
        gpre_ref, wga_ref, walt_ref, bcol_ref, h_ref, hn_ref, lat_ref = rest
        hn, lat = _norm_gate(h, gpre_ref[...], wga_ref[...], walt_ref[...], bcol_ref[...])
        hn_ref[...] = hn
        lat_ref[...] = lat
    else:
        (h_ref,) = rest
    h_ref[...] = h


def _out_proj(layer, og, oa, w_out, x, g_post, gate_args=None):
    m = x.shape[0]
    fused = gate_args is not None
    in_specs = [
        pl.BlockSpec((OUT_TM, D_GLA), lambda i: (i, 0)),
        pl.BlockSpec((OUT_TM, D_ATT), lambda i: (i, 0)),
        pl.BlockSpec((None, D_GLA, D_MODEL), lambda i: (layer, 0, 0)),
        pl.BlockSpec((None, D_ATT, D_MODEL), lambda i: (layer, 1, 0)),
        pl.BlockSpec((OUT_TM, D_MODEL), lambda i: (i, 0)),
        pl.BlockSpec((None, 1, D_MODEL), lambda i: (layer, 0, 0)),
    ]
    out_specs = [pl.BlockSpec((OUT_TM, D_MODEL), lambda i: (i, 0))]
    out_shape = [jax.ShapeDtypeStruct((m, D_MODEL), jnp.float32)]
    args = [og, oa, w_out, w_out, x, g_post]
    if fused:
        in_specs += _gate_specs(layer + 1, 1)
        out_specs += [pl.BlockSpec((OUT_TM, D_MODEL), lambda i: (i, 0)),
                      pl.BlockSpec((GLA_KW, OUT_TM), lambda i: (0, i))]
        out_shape += [jax.ShapeDtypeStruct((m, D_MODEL), jnp.bfloat16),
                      jax.ShapeDtypeStruct((GLA_KW, m), jnp.float32)]
        args += list(gate_args)
    return pl.pallas_call(
        functools.partial(_out_proj_kernel, fused),
        grid=(m // OUT_TM,),
        in_specs=in_specs,
        out_specs=out_specs,
        out_shape=out_shape,
        compiler_params=pltpu.CompilerParams(
            dimension_semantics=("parallel",), vmem_limit_bytes=VMEM_LIMIT),
        name="out_proj",
    )(*args)


def kernel(x, w_in, w_out, g_pre, g_post, w_alpha, b_alpha, g_gla, g_att, rel_bias):
    batch, seq, d_model = x.shape
    depth = w_in.shape[0]
    h = x.reshape(batch * seq, d_model)
    w_in_t = jnp.swapaxes(w_in, 1, 2).reshape(depth * D_IN, d_model)
    walt = jnp.pad(jnp.swapaxes(w_alpha, 1, 2), ((0, 0), (0, 0), (0, LANES - GATE_RANK))).astype(jnp.bfloat16)
    bcol = b_alpha[:, :, None]
    w_o = w_out.astype(jnp.bfloat16)
    bias = _band_bias_table(rel_bias)
    row = lambda p: p[:, None, :]
    gate_args = (row(g_pre), w_in_t, walt, bcol)
    hn, lat = _norm_gate_call(0, h, *gate_args)
    for l in range(depth):
        z = _proj(l, hn, w_in_t, transposed=False)
        kt = _proj(l, hn, w_in_t, transposed=True)
        og = _gla(l, z, kt, lat, row(g_gla), batch, seq)
        oa = _band_attn(l, z, kt, bias, row(g_att), batch, seq)
        if l + 1 < depth:
            h, hn, lat = _out_proj(l, og, oa, w_o, h, row(g_post), gate_args)
        else:
            (h,) = _out_proj(l, og, oa, w_o, h, row(g_post))
    return h.reshape(batch, seq, d_model)
```

```python
import functools
import math

import jax
import jax.numpy as jnp
import numpy as np
from jax import lax
from jax.experimental import pallas as pl
from jax.experimental.pallas import tpu as pltpu

D_MODEL = 2048
CHUNK = 64
D_GLA = 1024
D_ATT = 1024
GLA_HEADS = 4
GLA_DK = 128
GLA_DV = 256
GLA_KW = GLA_HEADS * GLA_DK
GATE_RANK = 16
GLA_TAU = 16.0
ATT_HEADS = 8
ATT_HD = 128
LEFT_CHUNKS = 8
REL_CLIP = 128
EPS = 1e-6
LOG2E = math.log2(math.e)

LANES = 128

W_GA = 2 * GLA_KW + 2 * D_GLA
W_AQ = W_GA + GATE_RANK
D_IN = W_AQ + 4 * D_ATT
ROW_ALIGN = GATE_RANK
PROJ_TN = 512
Z_GQ, Z_GV, Z_GG = 0, GLA_KW, GLA_KW + D_GLA
Z_AQ = GLA_KW + 2 * D_GLA
Z_AV, Z_AG = Z_AQ + D_ATT, Z_AQ + 2 * D_ATT
D_Z = Z_AG + D_ATT
KT_G, KT_A = 0, GLA_KW
D_KT = GLA_KW + D_ATT

PROJ_TM = 2048
NORM_TM = 512
OUT_TM = 256
GLA_GROUP = 256
GLA_UNROLL = 2
ATT_TQ = 128
ATT_WIN = ATT_TQ + LEFT_CHUNKS * CHUNK
CAST_ROWS = 256
VMEM_LIMIT = 48 * 1024 * 1024


def _silu(x):
    half = 0.5 * x
    return half + half * jnp.tanh(half)


def _log_sigmoid(x):
    return jnp.minimum(x, 0.0) - jnp.log(1.0 + jnp.exp(-jnp.abs(x)))


def _rms_scale(x):
    return lax.rsqrt(jnp.mean(x * x, axis=-1, keepdims=True) + EPS)


def _norm_kernel(x_ref, gpre_ref, hn_ref):
    x = x_ref[...]
    hn_ref[...] = (x * _rms_scale(x) * gpre_ref[...]).astype(hn_ref.dtype)


def _norm_call(layer, x, g_pre):
    m = x.shape[0]
    return pl.pallas_call(
        _norm_kernel,
        grid=(m // NORM_TM,),
        in_specs=[pl.BlockSpec((NORM_TM, D_MODEL), lambda i: (i, 0)),
                  pl.BlockSpec((None, 1, D_MODEL), lambda i: (layer, 0, 0))],
        out_specs=pl.BlockSpec((NORM_TM, D_MODEL), lambda i: (i, 0)),
        out_shape=jax.ShapeDtypeStruct((m, D_MODEL), jnp.bfloat16),
        compiler_params=pltpu.CompilerParams(
            dimension_semantics=("parallel",), vmem_limit_bytes=VMEM_LIMIT),
        name="norm",
    )(x, g_pre)


def _cast_rows(src_ref, dst_ref, n_rows):
    def body(r, carry):
        rows = pl.ds(pl.multiple_of(r * CAST_ROWS, CAST_ROWS), CAST_ROWS)
        dst_ref[rows, :] = src_ref[rows, :].astype(dst_ref.dtype)
        return carry
    lax.fori_loop(0, n_rows // CAST_ROWS, body, 0)


def _proj_nn_kernel(hn_ref, wt_ref, z_ref, wbf_ref):
    @pl.when(pl.program_id(1) == 0)
    def _():
        _cast_rows(wt_ref, wbf_ref, PROJ_TN)

    z_ref[...] = lax.dot_general(hn_ref[...], wbf_ref[...], (((1,), (1,)), ((), ())),
                                 preferred_element_type=jnp.float32).astype(z_ref.dtype)


def _proj_nt_kernel(hn_ref, wt_ref, wga_ref, kt_ref, gat_ref, wbf_ref):
    j = pl.program_id(0)

    @pl.when(pl.program_id(1) == 0)
    def _():
        _cast_rows(wt_ref, wbf_ref, PROJ_TN)

        @pl.when(j == 0)
        def _():
            wbf_ref[PROJ_TN:, :] = wga_ref[...].astype(wbf_ref.dtype)

    res = lax.dot_general(wbf_ref[...], hn_ref[...], (((1,), (1,)), ((), ())),
                          preferred_element_type=jnp.float32)
    scale = jnp.where(j == 0, GLA_DK ** -0.5, ATT_HD ** -0.5 * LOG2E)
    kt_ref[...] = (res[:PROJ_TN] * scale).astype(kt_ref.dtype)
    gat_ref[...] = res[PROJ_TN:].astype(gat_ref.dtype)


_NN_ROWS = ([0] + [2 * GLA_KW + t * PROJ_TN for t in range(2 * D_GLA // PROJ_TN)]
            + [W_AQ + t * PROJ_TN for t in range(D_ATT // PROJ_TN)]
            + [W_AQ + 2 * D_ATT + t * PROJ_TN for t in range(2 * D_ATT // PROJ_TN)])
_NT_ROWS = [GLA_KW] + [W_AQ + D_ATT + t * PROJ_TN for t in range(D_ATT // PROJ_TN)]


def _table_lookup(table, j):
    val = table[0] + j * 0
    for t in range(1, len(table)):
        val = jnp.where(j >= t, table[t], val)
    return val


def _w_rows_spec(layer, table, n_rows):
    units = [(layer * D_IN + r) // ROW_ALIGN for r in table]
    return pl.BlockSpec((pl.Element(n_rows), pl.Element(D_MODEL)),
                        lambda j, i: (_table_lookup(units, j) * ROW_ALIGN, 0))


def _proj_nn(layer, hn, w_in_t):
    m = hn.shape[0]
    return pl.pallas_call(
        _proj_nn_kernel,
        grid=(len(_NN_ROWS), m // PROJ_TM),
        in_specs=[pl.BlockSpec((PROJ_TM, D_MODEL), lambda j, i: (i, 0)),
                  _w_rows_spec(layer, _NN_ROWS, PROJ_TN)],
        out_specs=pl.BlockSpec((PROJ_TM, PROJ_TN), lambda j, i: (i, j)),
        out_shape=jax.ShapeDtypeStruct((m, D_Z), jnp.bfloat16),
        scratch_shapes=[pltpu.VMEM((PROJ_TN, D_MODEL), jnp.bfloat16)],
        compiler_params=pltpu.CompilerParams(
            dimension_semantics=("arbitrary", "arbitrary"), vmem_limit_bytes=VMEM_LIMIT),
        name="proj_nn",
    )(hn, w_in_t)


def _proj_nt(layer, hn, w_in_t):
    m = hn.shape[0]
    return pl.pallas_call(
        _proj_nt_kernel,
        grid=(len(_NT_ROWS), m // PROJ_TM),
        in_specs=[pl.BlockSpec((PROJ_TM, D_MODEL), lambda j, i: (i, 0)),
                  _w_rows_spec(layer, _NT_ROWS, PROJ_TN),
                  _w_rows_spec(layer, [W_GA], GATE_RANK)],
        out_specs=[pl.BlockSpec((PROJ_TN, PROJ_TM), lambda j, i: (j, i)),
                   pl.BlockSpec((None, GATE_RANK, PROJ_TM), lambda j, i: (j, 0, i))],
        out_shape=[jax.ShapeDtypeStruct((D_KT, m), jnp.bfloat16),
                   jax.ShapeDtypeStruct((len(_NT_ROWS), GATE_RANK, m), jnp.bfloat16)],
        scratch_shapes=[pltpu.VMEM((PROJ_TN + GATE_RANK, D_MODEL), jnp.bfloat16)],
        compiler_params=pltpu.CompilerParams(
            dimension_semantics=("arbitrary", "arbitrary"), vmem_limit_bytes=VMEM_LIMIT),
        name="proj_nt",
    )(hn, w_in_t, w_in_t)


def _split3_bf16(x):
    hi = x.astype(jnp.bfloat16)
    r1 = x - hi.astype(jnp.float32)
    mid = r1.astype(jnp.bfloat16)
    lo = (r1 - mid.astype(jnp.float32)).astype(jnp.bfloat16)
    return hi, mid, lo


def _gla_kernel(q_ref, kt_ref, v_ref, gate_ref, gat_ref, walt_ref, bcol_ref, g_ref, o_ref, st_ref, *scratch):
    seq = q_ref.shape[0]
    n_groups = seq // GLA_GROUP
    chunks_per_group = GLA_GROUP // CHUNK
    parts_ref, sums_ref, u_ref, a_ref = scratch
    r = lax.broadcasted_iota(jnp.int32, (3 * GLA_GROUP, 2 * GLA_GROUP), 0) % GLA_GROUP
    c = lax.broadcasted_iota(jnp.int32, (3 * GLA_GROUP, 2 * GLA_GROUP), 1)
    sums = jnp.where((r // CHUNK == (c % GLA_GROUP) // CHUNK) & (r <= c), 1.0, 0.0).astype(jnp.bfloat16)
    frame_chunk = lax.broadcasted_iota(jnp.int32, (GLA_DK, GLA_GROUP), 1) // CHUNK
    ga_pad = jnp.zeros((LANES - GATE_RANK, GLA_GROUP), jnp.bfloat16)
    g = g_ref[...]

    def group_cols(gi):
        return pl.ds(pl.multiple_of(gi * GLA_GROUP, GLA_GROUP), GLA_GROUP)

    def gate_step(gi, carry):
        ga = jnp.concatenate([gat_ref[:, group_cols(gi)], ga_pad], axis=0)
        pre = jnp.dot(walt_ref[...], ga, preferred_element_type=jnp.float32) + bcol_ref[...]
        log_a = _log_sigmoid(pre) * (1.0 / GLA_TAU)
        rows = pl.ds(pl.multiple_of(gi * GLA_DK, GLA_DK), GLA_DK)
        parts_ref[rows, :] = jnp.concatenate(_split3_bf16(log_a), axis=1)
        return carry

    def decay_step(gi, carry):
        cols = group_cols(gi)
        both = sums_ref[pl.ds(pl.multiple_of(gi * GLA_DK, GLA_DK), GLA_DK), :]
        run, total = both[:, :GLA_GROUP], both[:, GLA_GROUP:]
        k_dec = (kt_ref[:, cols].astype(jnp.float32) * jnp.exp(total - run)).astype(jnp.bfloat16)
        k_own = jnp.concatenate([jnp.where(frame_chunk == ci, k_dec, jnp.zeros_like(k_dec))
                                 for ci in range(chunks_per_group)], axis=0)
        u = jnp.dot(k_own, v_ref[cols, :], preferred_element_type=jnp.float32)
        a = jnp.exp(total)
        for ci in range(chunks_per_group):
            u_ref[gi, ci] = u[ci * GLA_DK:(ci + 1) * GLA_DK]
            a_ref[gi, ci] = jnp.broadcast_to(a[:, ci * CHUNK:ci * CHUNK + 1], (GLA_DK, GLA_DV))
        return carry

    def state_step(gi, carry):
        base = pl.multiple_of(gi * GLA_GROUP, GLA_GROUP)
        s_c = st_ref[...]
        for ci in range(chunks_per_group):
            s_c = a_ref[gi, ci] * s_c + u_ref[gi, ci]
            rows = pl.ds(base + ci * CHUNK, CHUNK)
            o = jnp.dot(q_ref[rows, :], s_c.astype(jnp.bfloat16), preferred_element_type=jnp.float32)
            o = o * _rms_scale(o) * g
            o_ref[rows, :] = (o * _silu(gate_ref[rows, :].astype(jnp.float32))).astype(o_ref.dtype)
        st_ref[...] = s_c
        return carry

    st_ref[...] = jnp.zeros_like(st_ref)
    lax.fori_loop(0, n_groups, gate_step, 0, unroll=2 * GLA_UNROLL)
    sums_ref[...] = jnp.dot(parts_ref[...], sums, preferred_element_type=jnp.float32)
    lax.fori_loop(0, n_groups, decay_step, 0, unroll=2 * GLA_UNROLL)
    lax.fori_loop(0, n_groups, state_step, 0, unroll=GLA_UNROLL)


def _gla(layer, z, kt, gat, walt, bcol, g_gla, batch, seq):
    m = z.shape[0]
    chunks_per_group = GLA_GROUP // CHUNK
    n_groups = seq // GLA_GROUP
    assert n_groups % GLA_UNROLL == 0
    return pl.pallas_call(
        _gla_kernel,
        grid=(batch, GLA_HEADS),
        in_specs=[
            pl.BlockSpec((seq, GLA_DK), lambda b, h: (b, Z_GQ // GLA_DK + h)),
            pl.BlockSpec((GLA_DK, seq), lambda b, h: (KT_G // GLA_DK + h, b)),
            pl.BlockSpec((seq, GLA_DV), lambda b, h: (b, Z_GV // GLA_DV + h)),
            pl.BlockSpec((seq, GLA_DV), lambda b, h: (b, Z_GG // GLA_DV + h)),
            pl.BlockSpec((None, GATE_RANK, seq), lambda b, h: (0, 0, b)),
            pl.BlockSpec((None, GLA_DK, LANES), lambda b, h: (layer, h, 0)),
            pl.BlockSpec((None, GLA_DK, 1), lambda b, h: (layer, h, 0)),
            pl.BlockSpec((None, 1, GLA_DV), lambda b, h: (layer, 0, h)),
        ],
        out_specs=pl.BlockSpec((seq, GLA_DV), lambda b, h: (b, h)),
        out_shape=jax.ShapeDtypeStruct((m, D_GLA), jnp.bfloat16),
        scratch_shapes=[pltpu.VMEM((GLA_DK, GLA_DV), jnp.float32),
                        pltpu.VMEM((n_groups * GLA_DK, 3 * GLA_GROUP), jnp.bfloat16),
                        pltpu.VMEM((n_groups * GLA_DK, 2 * GLA_GROUP), jnp.float32),
                        pltpu.VMEM((n_groups, chunks_per_group, GLA_DK, GLA_DV), jnp.float32),
                        pltpu.VMEM((n_groups, chunks_per_group, GLA_DK, GLA_DV), jnp.float32)],
        compiler_params=pltpu.CompilerParams(
            dimension_semantics=("parallel", "parallel"), vmem_limit_bytes=VMEM_LIMIT),
        name="gla",
    )(z, kt, z, z, gat, walt, bcol, g_gla)


def _attn_scores(q_ref, kt_ref, bias_ref, q_start, k_start, width):
    s = jnp.dot(q_ref[q_start:q_start + ATT_TQ, :], kt_ref[:, k_start:k_start + width],
                preferred_element_type=jnp.float32)
    return s + bias_ref[0, :, ATT_WIN - width:]


def _attn_softmax(s2):
    m = jnp.max(s2, axis=-1, keepdims=True)
    e = jnp.exp2(s2 - m)
    return e.astype(jnp.bfloat16), jnp.sum(e, axis=-1, keepdims=True)


def _attn_finish(p, denom, v_ref, gate_ref, g, o_ref, q_start, k_start, width):
    o = jnp.dot(p, v_ref[k_start:k_start + width, :], preferred_element_type=jnp.float32)
    o = o * (1.0 / denom)
    o = o * _rms_scale(o) * g
    rows = slice(q_start, q_start + ATT_TQ)
    o_ref[rows, :] = (o * _silu(gate_ref[rows, :].astype(jnp.float32))).astype(o_ref.dtype)


def _attn_kernel(q_ref, kt_ref, v_ref, gate_ref, bias_ref, g_ref, o_ref):
    seq = q_ref.shape[0]
    g = g_ref[...]
    n_tiles = seq // ATT_TQ

    def window(t):
        k_start = max(0, (t + 1) * ATT_TQ - ATT_WIN)
        return t * ATT_TQ, k_start, (t + 1) * ATT_TQ - k_start

    scores, probs = {}, {}
    for step in range(n_tiles + 2):
        if step < n_tiles:
            scores[step] = _attn_scores(q_ref, kt_ref, bias_ref, *window(step))
        if 0 <= step - 1 < n_tiles:
            probs[step - 1] = _attn_softmax(scores.pop(step - 1))
        if 0 <= step - 2:
            _attn_finish(*probs.pop(step - 2), v_ref, gate_ref, g, o_ref, *window(step - 2))


def _band_attn(layer, z, kt, bias, g_att, batch, seq):
    m = z.shape[0]
    blk = lambda off: off // ATT_HD
    return pl.pallas_call(
        _attn_kernel,
        grid=(batch, ATT_HEADS),
        in_specs=[
            pl.BlockSpec((seq, ATT_HD), lambda b, h: (b, blk(Z_AQ) + h)),
            pl.BlockSpec((ATT_HD, seq), lambda b, h: (blk(KT_A) + h, b)),
            pl.BlockSpec((seq, ATT_HD), lambda b, h: (b, blk(Z_AV) + h)),
            pl.BlockSpec((seq, ATT_HD), lambda b, h: (b, blk(Z_AG) + h)),
            pl.BlockSpec((None, 1, ATT_TQ, ATT_WIN), lambda b, h: (layer, h, 0, 0)),
            pl.BlockSpec((None, 1, ATT_HD), lambda b, h: (layer, 0, h)),
        ],
        out_specs=pl.BlockSpec((seq, ATT_HD), lambda b, h: (b, h)),
        out_shape=jax.ShapeDtypeStruct((m, D_ATT), jnp.bfloat16),
        compiler_params=pltpu.CompilerParams(
            dimension_semantics=("parallel", "parallel"), vmem_limit_bytes=VMEM_LIMIT),
        name="band_attn",
    )(z, kt, z, z, bias, g_att)


def _band_bias_table(rel_bias):
    lead = rel_bias.shape[:-1]
    span = ATT_TQ + ATT_WIN - 1
    n_far = ATT_WIN - REL_CLIP
    rb = rel_bias.astype(jnp.float32) * LOG2E
    c = jnp.concatenate([jnp.broadcast_to(rb[..., 2 * REL_CLIP:], lead + (n_far,)),
                         rb[..., 2 * REL_CLIP - 1:0:-1]], axis=-1)
    cp = jnp.pad(c, [(0, 0)] * len(lead) + [(0, 1)])
    skew = jnp.tile(cp, ATT_TQ)[..., :ATT_TQ * span].reshape(lead + (ATT_TQ, span))
    table = skew[..., ATT_TQ - 1:]
    i = np.arange(ATT_TQ)[:, None]
    j = np.arange(ATT_WIN)[None, :]
    first = (i // CHUNK) * CHUNK
    in_band = (j >= first) & (j < first + (LEFT_CHUNKS + 1) * CHUNK)
    return jnp.where(in_band, table, -1e30)


def _out_proj_kernel(fused, og_ref, oa_ref, wg_ref, wa_ref, x_ref, gpost_ref, *rest):
    y = (jnp.dot(og_ref[...], wg_ref[...], preferred_element_type=jnp.float32)
         + jnp.dot(oa_ref[...], wa_ref[...], preferred_element_type=jnp.float32))
    h = x_ref[...] + y * _rms_scale(y) * gpost_ref[...]
    if fused:
        gpre_ref, h_ref, hn_ref = rest
        hn_ref[...] = (h * _rms_scale(h) * gpre_ref[...]).astype(hn_ref.dtype)
    else:
        (h_ref,) = rest
    h_ref[...] = h


def _out_proj(layer, og, oa, w_out, x, g_post, g_pre=None):
    m = x.shape[0]
    fused = g_pre is not None
    in_specs = [
        pl.BlockSpec((OUT_TM, D_GLA), lambda i: (i, 0)),
        pl.BlockSpec((OUT_TM, D_ATT), lambda i: (i, 0)),
        pl.BlockSpec((None, D_GLA, D_MODEL), lambda i: (layer, 0, 0)),
        pl.BlockSpec((None, D_ATT, D_MODEL), lambda i: (layer, 1, 0)),
        pl.BlockSpec((OUT_TM, D_MODEL), lambda i: (i, 0)),
        pl.BlockSpec((None, 1, D_MODEL), lambda i: (layer, 0, 0)),
    ]
    out_specs = [pl.BlockSpec((OUT_TM, D_MODEL), lambda i: (i, 0))]
    out_shape = [jax.ShapeDtypeStruct((m, D_MODEL), jnp.float32)]
    args = [og, oa, w_out, w_out, x, g_post]
    if fused:
        in_specs.append(pl.BlockSpec((None, 1, D_MODEL), lambda i: (layer + 1, 0, 0)))
        out_specs.append(pl.BlockSpec((OUT_TM, D_MODEL), lambda i: (i, 0)))
        out_shape.append(jax.ShapeDtypeStruct((m, D_MODEL), jnp.bfloat16))
        args.append(g_pre)
    return pl.pallas_call(
        functools.partial(_out_proj_kernel, fused),
        grid=(m // OUT_TM,),
        in_specs=in_specs,
        out_specs=out_specs,
        out_shape=out_shape,
        compiler_params=pltpu.CompilerParams(
            dimension_semantics=("parallel",), vmem_limit_bytes=VMEM_LIMIT),
        name="out_proj",
    )(*args)


def kernel(x, w_in, w_out, g_pre, g_post, w_alpha, b_alpha, g_gla, g_att, rel_bias):
    batch, seq, d_model = x.shape
    depth = w_in.shape[0]
    h = x.reshape(batch * seq, d_model)
    w_in_t = jnp.swapaxes(w_in, 1, 2).reshape(depth * D_IN, d_model)
    walt = jnp.pad(jnp.swapaxes(w_alpha, 1, 2), ((0, 0), (0, 0), (0, LANES - GATE_RANK))).astype(jnp.bfloat16)
    bcol = b_alpha[:, :, None]
    w_o = w_out.astype(jnp.bfloat16)
    bias = _band_bias_table(rel_bias)
    row = lambda p: p[:, None, :]
    hn = _norm_call(0, h, row(g_pre))
    for l in range(depth):
        z = _proj_nn(l, hn, w_in_t)
        kt, gat = _proj_nt(l, hn, w_in_t)
        og = _gla(l, z, kt, gat, walt, bcol, row(g_gla), batch, seq)
        oa = _band_attn(l, z, kt, bias, row(g_att), batch, seq)
        if l + 1 < depth:
            h, hn = _out_proj(l, og, oa, w_o, h, row(g_post), row(g_pre))
        else:
            (h,) = _out_proj(l, og, oa, w_o, h, row(g_post))
    return h.reshape(batch, seq, d_model)
```

```python
import functools
import math

import jax
import jax.numpy as jnp
import numpy as np
from jax import lax
from jax.experimental import pallas as pl
from jax.experimental.pallas import tpu as pltpu

D_MODEL = 2048
CHUNK = 64
D_GLA = 1024
D_ATT = 1024
GLA_HEADS = 4
GLA_DK = 128
GLA_DV = 256
GLA_KW = GLA_HEADS * GLA_DK
GATE_RANK = 16
GLA_TAU = 16.0
ATT_HEADS = 8
ATT_HD = 128
LEFT_CHUNKS = 8
REL_CLIP = 128
EPS = 1e-6
LOG2E = math.log2(math.e)

LANES = 128
N_STREAMS = 2

W_GA = 2 * GLA_KW + 2 * D_GLA
W_AQ = W_GA + GATE_RANK
D_IN = W_AQ + 4 * D_ATT
ROW_ALIGN = GATE_RANK
PROJ_TN = 512
Z_GQ, Z_GV, Z_GG = 0, GLA_KW, GLA_KW + D_GLA
Z_AQ = GLA_KW + 2 * D_GLA
Z_AV, Z_AG = Z_AQ + D_ATT, Z_AQ + 2 * D_ATT
D_Z = Z_AG + D_ATT
KT_G, KT_A = 0, GLA_KW
D_KT = GLA_KW + D_ATT

PROJ_TM = 1024
NN_PIECE_ROWS, NN_PIECE_COLS = 512, 256
NORM_TM = 512
OUT_TM = 256
GLA_GROUP = 256
GLA_UNROLL = 2
ATT_TQ = 128
ATT_WIN = ATT_TQ + LEFT_CHUNKS * CHUNK
CAST_ROWS = 256
VMEM_LIMIT = 48 * 1024 * 1024
VMEM_LIMIT_FUSED = 56 * 1024 * 1024


def _silu(x):
    half = 0.5 * x
    return half + half * jnp.tanh(half)


def _log_sigmoid(x):
    return jnp.minimum(x, 0.0) - jnp.log(1.0 + jnp.exp(-jnp.abs(x)))


def _rms_scale(x):
    return lax.rsqrt(jnp.mean(x * x, axis=-1, keepdims=True) + EPS)


def _norm_kernel(x_ref, gpre_ref, hn_ref):
    x = x_ref[...]
    hn_ref[...] = (x * _rms_scale(x) * gpre_ref[...]).astype(hn_ref.dtype)


def _norm_call(layer, x, row0, rows, g_pre):
    blk0 = row0 // NORM_TM
    return pl.pallas_call(
        _norm_kernel,
        grid=(rows // NORM_TM,),
        in_specs=[pl.BlockSpec((NORM_TM, D_MODEL), lambda i: (blk0 + i, 0)),
                  pl.BlockSpec((None, 1, D_MODEL), lambda i: (layer, 0, 0))],
        out_specs=pl.BlockSpec((NORM_TM, D_MODEL), lambda i: (i, 0)),
        out_shape=jax.ShapeDtypeStruct((rows, D_MODEL), jnp.bfloat16),
        compiler_params=pltpu.CompilerParams(
            dimension_semantics=("parallel",), vmem_limit_bytes=VMEM_LIMIT),
        name="norm",
    )(x, g_pre)


def _split3_bf16(x):
    hi = x.astype(jnp.bfloat16)
    r1 = x - hi.astype(jnp.float32)
    mid = r1.astype(jnp.bfloat16)
    lo = (r1 - mid.astype(jnp.float32)).astype(jnp.bfloat16)
    return hi, mid, lo


def _gla_unit(straight_line, q_ref, kt_ref, v_ref, gate_ref, gat_ref, walt_ref, bcol_ref, g_ref, o_ref,
              st_ref, parts_ref, sums_ref, u_ref, a_ref):
    seq = q_ref.shape[0]
    n_groups = seq // GLA_GROUP
    chunks_per_group = GLA_GROUP // CHUNK
    r = lax.broadcasted_iota(jnp.int32, (3 * GLA_GROUP, 2 * GLA_GROUP), 0) % GLA_GROUP
    c = lax.broadcasted_iota(jnp.int32, (3 * GLA_GROUP, 2 * GLA_GROUP), 1)
    sums = jnp.where((r // CHUNK == (c % GLA_GROUP) // CHUNK) & (r <= c), 1.0, 0.0).astype(jnp.bfloat16)
    frame_chunk = lax.broadcasted_iota(jnp.int32, (GLA_DK, GLA_GROUP), 1) // CHUNK
    ga_pad = jnp.zeros((LANES - GATE_RANK, GLA_GROUP), jnp.bfloat16)
    g = g_ref[...]

    def aligned(start, size):
        return pl.ds(start if isinstance(start, int) else pl.multiple_of(start, size), size)

    def gate_step(gi, carry):
        ga = jnp.concatenate([gat_ref[:, aligned(gi * GLA_GROUP, GLA_GROUP)], ga_pad], axis=0)
        pre = jnp.dot(walt_ref[...], ga, preferred_element_type=jnp.float32) + bcol_ref[...]
        log_a = _log_sigmoid(pre) * (1.0 / GLA_TAU)
        parts_ref[aligned(gi * GLA_DK, GLA_DK), :] = jnp.concatenate(_split3_bf16(log_a), axis=1)
        return carry

    def decay_step(gi, carry):
        cols = aligned(gi * GLA_GROUP, GLA_GROUP)
        both = sums_ref[aligned(gi * GLA_DK, GLA_DK), :]
        run, total = both[:, :GLA_GROUP], both[:, GLA_GROUP:]
        k_dec = (kt_ref[:, cols].astype(jnp.float32) * jnp.exp(total - run)).astype(jnp.bfloat16)
        k_own = jnp.concatenate([jnp.where(frame_chunk == ci, k_dec, jnp.zeros_like(k_dec))
                                 for ci in range(chunks_per_group)], axis=0)
        u = jnp.dot(k_own, v_ref[cols, :], preferred_element_type=jnp.float32)
        a = jnp.exp(total)
        for ci in range(chunks_per_group):
            u_ref[gi, ci] = u[ci * GLA_DK:(ci + 1) * GLA_DK]
            a_ref[gi, ci] = jnp.broadcast_to(a[:, ci * CHUNK:ci * CHUNK + 1], (GLA_DK, GLA_DV))
        return carry

    def state_step(gi, carry):
        s_c = st_ref[...]
        for ci in range(chunks_per_group):
            s_c = a_ref[gi, ci] * s_c + u_ref[gi, ci]
            rows = aligned(gi * GLA_GROUP + ci * CHUNK, CHUNK)
            o = jnp.dot(q_ref[rows, :], s_c.astype(jnp.bfloat16), preferred_element_type=jnp.float32)
            o = o * _rms_scale(o) * g
            o_ref[rows, :] = (o * _silu(gate_ref[rows, :].astype(jnp.float32))).astype(o_ref.dtype)
        st_ref[...] = s_c
        return carry

    def run_pass(step, unroll):
        if straight_line:
            for gi in range(n_groups):
                step(gi, 0)
        else:
            lax.fori_loop(0, n_groups, step, 0, unroll=unroll)

    st_ref[...] = jnp.zeros_like(st_ref)
    run_pass(gate_step, 2 * GLA_UNROLL)
    sums_ref[...] = jnp.dot(parts_ref[...], sums, preferred_element_type=jnp.float32)
    run_pass(decay_step, 2 * GLA_UNROLL)
    run_pass(state_step, GLA_UNROLL)


def _gla_specs(layer, seq, unit):
    b_h = lambda off: (lambda *idx: (unit(*idx)[0], off + unit(*idx)[1]))
    in_specs = [
        pl.BlockSpec((seq, GLA_DK), b_h(Z_GQ // GLA_DK)),
        pl.BlockSpec((GLA_DK, seq), lambda *idx: (KT_G // GLA_DK + unit(*idx)[1], unit(*idx)[0])),
        pl.BlockSpec((seq, GLA_DV), b_h(Z_GV // GLA_DV)),
        pl.BlockSpec((seq, GLA_DV), b_h(Z_GG // GLA_DV)),
        pl.BlockSpec((None, GATE_RANK, seq), lambda *idx: (0, 0, unit(*idx)[0])),
        pl.BlockSpec((None, GLA_DK, LANES), lambda *idx: (layer, unit(*idx)[1], 0)),
        pl.BlockSpec((None, GLA_DK, 1), lambda *idx: (layer, unit(*idx)[1], 0)),
        pl.BlockSpec((None, 1, GLA_DV), lambda *idx: (layer, 0, unit(*idx)[1])),
    ]
    out_spec = pl.BlockSpec((seq, GLA_DV), lambda *idx: unit(*idx))
    return in_specs, out_spec


def _gla_scratch(seq):
    n_groups = seq // GLA_GROUP
    chunks_per_group = GLA_GROUP // CHUNK
    return [pltpu.VMEM((GLA_DK, GLA_DV), jnp.float32),
            pltpu.VMEM((n_groups * GLA_DK, 3 * GLA_GROUP), jnp.bfloat16),
            pltpu.VMEM((n_groups * GLA_DK, 2 * GLA_GROUP), jnp.float32),
            pltpu.VMEM((n_groups, chunks_per_group, GLA_DK, GLA_DV), jnp.float32),
            pltpu.VMEM((n_groups, chunks_per_group, GLA_DK, GLA_DV), jnp.float32)]


def _gla(layer, z, kt, gat, walt, bcol, g_gla, batch, seq):
    assert (seq // GLA_GROUP) % (2 * GLA_UNROLL) == 0
    in_specs, out_spec = _gla_specs(layer, seq, lambda b, h: (b, h))
    return pl.pallas_call(
        functools.partial(_gla_unit, False),
        grid=(batch, GLA_HEADS),
        in_specs=in_specs,
        out_specs=out_spec,
        out_shape=jax.ShapeDtypeStruct((z.shape[0], D_GLA), jnp.bfloat16),
        scratch_shapes=_gla_scratch(seq),
        compiler_params=pltpu.CompilerParams(
            dimension_semantics=("parallel", "parallel"), vmem_limit_bytes=VMEM_LIMIT),
        name="gla",
    )(z, kt, z, z, gat, walt, bcol, g_gla)


def _attn_scores(q_ref, kt_ref, bias_ref, q_start, k_start, width):
    s = jnp.dot(q_ref[q_start:q_start + ATT_TQ, :], kt_ref[:, k_start:k_start + width],
                preferred_element_type=jnp.float32)
    return s + bias_ref[0, :, ATT_WIN - width:]


def _attn_softmax(s2):
    m = jnp.max(s2, axis=-1, keepdims=True)
    e = jnp.exp2(s2 - m)
    return e.astype(jnp.bfloat16), jnp.sum(e, axis=-1, keepdims=True)


def _attn_finish(p, denom, v_ref, gate_ref, g, o_ref, q_start, k_start, width):
    o = jnp.dot(p, v_ref[k_start:k_start + width, :], preferred_element_type=jnp.float32)
    o = o * (1.0 / denom)
    o = o * _rms_scale(o) * g
    rows = slice(q_start, q_start + ATT_TQ)
    o_ref[rows, :] = (o * _silu(gate_ref[rows, :].astype(jnp.float32))).astype(o_ref.dtype)


def _attn_unit(q_ref, kt_ref, v_ref, gate_ref, bias_ref, g_ref, o_ref, side_work=()):
    seq = q_ref.shape[0]
    g = g_ref[...]
    n_tiles = seq // ATT_TQ
    side_at = {(k * (n_tiles + 2)) // len(side_work): fn for k, fn in enumerate(side_work)} if side_work else {}

    def window(t):
        k_start = max(0, (t + 1) * ATT_TQ - ATT_WIN)
        return t * ATT_TQ, k_start, (t + 1) * ATT_TQ - k_start

    scores, probs = {}, {}
    for step in range(n_tiles + 2):
        if step in side_at:
            side_at[step]()
        if step < n_tiles:
            scores[step] = _attn_scores(q_ref, kt_ref, bias_ref, *window(step))
        if 0 <= step - 1 < n_tiles:
            probs[step - 1] = _attn_softmax(scores.pop(step - 1))
        if 0 <= step - 2:
            _attn_finish(*probs.pop(step - 2), v_ref, gate_ref, g, o_ref, *window(step - 2))


def _attn_specs(layer, seq, unit):
    blk = lambda off: off // ATT_HD
    b_h = lambda off: (lambda *idx: (unit(*idx)[0], off + unit(*idx)[1]))
    in_specs = [
        pl.BlockSpec((seq, ATT_HD), b_h(blk(Z_AQ))),
        pl.BlockSpec((ATT_HD, seq), lambda *idx: (blk(KT_A) + unit(*idx)[1], unit(*idx)[0])),
        pl.BlockSpec((seq, ATT_HD), b_h(blk(Z_AV))),
        pl.BlockSpec((seq, ATT_HD), b_h(blk(Z_AG))),
        pl.BlockSpec((None, 1, ATT_TQ, ATT_WIN), lambda *idx: (layer, unit(*idx)[1], 0, 0)),
        pl.BlockSpec((None, 1, ATT_HD), lambda *idx: (layer, 0, unit(*idx)[1])),
    ]
    out_spec = pl.BlockSpec((seq, ATT_HD), lambda *idx: unit(*idx))
    return in_specs, out_spec


def _band_attn(layer, z, kt, bias, g_att, batch, seq):
    in_specs, out_spec = _attn_specs(layer, seq, lambda b, h: (b, h))
    return pl.pallas_call(
        _attn_unit,
        grid=(batch, ATT_HEADS),
        in_specs=in_specs,
        out_specs=out_spec,
        out_shape=jax.ShapeDtypeStruct((z.shape[0], D_ATT), jnp.bfloat16),
        compiler_params=pltpu.CompilerParams(
            dimension_semantics=("parallel", "parallel"), vmem_limit_bytes=VMEM_LIMIT),
        name="band_attn",
    )(z, kt, z, z, bias, g_att)


def _band_bias_table(rel_bias):
    lead = rel_bias.shape[:-1]
    span = ATT_TQ + ATT_WIN - 1
    n_far = ATT_WIN - REL_CLIP
    rb = rel_bias.astype(jnp.float32) * LOG2E
    c = jnp.concatenate([jnp.broadcast_to(rb[..., 2 * REL_CLIP:], lead + (n_far,)),
                         rb[..., 2 * REL_CLIP - 1:0:-1]], axis=-1)
    cp = jnp.pad(c, [(0, 0)] * len(lead) + [(0, 1)])
    skew = jnp.tile(cp, ATT_TQ)[..., :ATT_TQ * span].reshape(lead + (ATT_TQ, span))
    table = skew[..., ATT_TQ - 1:]
    i = np.arange(ATT_TQ)[:, None]
    j = np.arange(ATT_WIN)[None, :]
    first = (i // CHUNK) * CHUNK
    in_band = (j >= first) & (j < first + (LEFT_CHUNKS + 1) * CHUNK)
    return jnp.where(in_band, table, -1e30)


_NN_ROWS = ([0] + [2 * GLA_KW + t * PROJ_TN for t in range(2 * D_GLA // PROJ_TN)]
            + [W_AQ + t * PROJ_TN for t in range(D_ATT // PROJ_TN)]
            + [W_AQ + 2 * D_ATT + t * PROJ_TN for t in range(2 * D_ATT // PROJ_TN)])
_NT_ROWS = [GLA_KW] + [W_AQ + D_ATT + t * PROJ_TN for t in range(D_ATT // PROJ_TN)]


def _table_lookup(table, j):
    val = table[0] + j * 0
    for t in range(1, len(table)):
        val = jnp.where(j >= t, table[t], val)
    return val


def _proj_kernel(plan, *refs):
    hn_ref, wt_ref, wga_ref = refs[:3]
    n_in = 3 + (14 if plan.mix else 0)
    z_ref, kt_ref, gat_ref = refs[n_in:n_in + 3]
    n_out = n_in + 3 + (2 if plan.mix else 0)
    wbf_ref = refs[n_out]
    s = pl.program_id(0)

    @pl.when(s % plan.tiles == 0)
    def _():
        def body(r, carry):
            rows = pl.ds(pl.multiple_of(r * CAST_ROWS, CAST_ROWS), CAST_ROWS)
            wbf_ref[rows, :] = wt_ref[rows, :].astype(wbf_ref.dtype)
            return carry
        lax.fori_loop(0, PROJ_TN // CAST_ROWS, body, 0)

    @pl.when(s == plan.nn_steps)
    def _():
        wbf_ref[PROJ_TN:, :] = wga_ref[...].astype(wbf_ref.dtype)

    def matmul_nn():
        z_ref[...] = lax.dot_general(hn_ref[...], wbf_ref[:PROJ_TN, :], (((1,), (1,)), ((), ())),
                                     preferred_element_type=jnp.float32).astype(z_ref.dtype)

    def matmul_nt():
        res = lax.dot_general(wbf_ref[...], hn_ref[...], (((1,), (1,)), ((), ())),
                              preferred_element_type=jnp.float32)
        scale = jnp.where(s < plan.nn_steps + plan.tiles, GLA_DK ** -0.5, ATT_HD ** -0.5 * LOG2E)
        kt_ref[...] = (res[:PROJ_TN] * scale).astype(kt_ref.dtype)
        gat_ref[...] = res[PROJ_TN:].astype(gat_ref.dtype)

    if plan.mix:
        attn_refs = refs[3:9] + (refs[n_in + 3],)
        gla_refs = refs[9:17] + (refs[n_in + 4],) + refs[n_out + 1:]
        n_mix = plan.attn_units + plan.gla_units

        def nn_piece(r0, c0):
            def run():
                rows, cols = slice(r0, r0 + NN_PIECE_ROWS), slice(c0, c0 + NN_PIECE_COLS)
                z_ref[rows, cols] = lax.dot_general(hn_ref[rows, :], wbf_ref[cols, :], (((1,), (1,)), ((), ())),
                                                    preferred_element_type=jnp.float32).astype(z_ref.dtype)
            return run

        @pl.when(s < plan.attn_units)
        def _():
            pieces = [nn_piece(r0, c0) for r0 in range(0, PROJ_TM, NN_PIECE_ROWS)
                      for c0 in range(0, PROJ_TN, NN_PIECE_COLS)]
            _attn_unit(*attn_refs, side_work=pieces)

        @pl.when((s >= plan.attn_units) & (s < n_mix))
        def _():
            matmul_nn()
            _gla_unit(True, *gla_refs)

        pl.when((s >= n_mix) & (s < plan.nn_steps))(matmul_nn)
    else:
        pl.when(s < plan.nn_steps)(matmul_nn)
    pl.when(s >= plan.nn_steps)(matmul_nt)


class _ProjPlan:
    def __init__(self, rows, batch, mix):
        self.tiles = rows // PROJ_TM
        self.nn_steps = len(_NN_ROWS) * self.tiles
        self.nt_steps = len(_NT_ROWS) * self.tiles
        self.mix = mix
        self.attn_units = batch * ATT_HEADS
        self.gla_units = batch * GLA_HEADS
        assert not mix or self.attn_units + self.gla_units <= self.nn_steps


def _proj(layer, hn, w_in_t, batch, seq, mix=None):
    rows = hn.shape[0]
    plan = _ProjPlan(rows, batch, mix is not None)
    t = plan.tiles
    w_units = [(layer * D_IN + r) // ROW_ALIGN for r in _NN_ROWS + _NT_ROWS]
    nn_last = plan.nn_steps - 1

    def nt_step(s):
        return jnp.maximum(s - plan.nn_steps, 0)

    in_specs = [
        pl.BlockSpec((PROJ_TM, D_MODEL), lambda s: (s % t, 0)),
        pl.BlockSpec((pl.Element(PROJ_TN), pl.Element(D_MODEL)), lambda s: (_table_lookup(w_units, s // t) * ROW_ALIGN, 0)),
        pl.BlockSpec((pl.Element(GATE_RANK), pl.Element(D_MODEL)), lambda s: (layer * D_IN + W_GA, 0)),
    ]
    out_specs = [
        pl.BlockSpec((PROJ_TM, PROJ_TN), lambda s: (jnp.minimum(s, nn_last) % t, jnp.minimum(s, nn_last) // t)),
        pl.BlockSpec((PROJ_TN, PROJ_TM), lambda s: (nt_step(s) // t, nt_step(s) % t)),
        pl.BlockSpec((None, GATE_RANK, PROJ_TM), lambda s: (nt_step(s) // t, 0, nt_step(s) % t)),
    ]
    out_shape = [jax.ShapeDtypeStruct((rows, D_Z), jnp.bfloat16),
                 jax.ShapeDtypeStruct((D_KT, rows), jnp.bfloat16),
                 jax.ShapeDtypeStruct((len(_NT_ROWS), GATE_RANK, rows), jnp.bfloat16)]
    scratch = [pltpu.VMEM((PROJ_TN + GATE_RANK, D_MODEL), jnp.bfloat16)]
    args = [hn, w_in_t, w_in_t]
    if mix is not None:
        mlayer, mz, mkt, mgat, walt, bcol, g_gla, bias, g_att = mix

        def attn_unit(s):
            u = jnp.minimum(s, plan.attn_units - 1)
            return u // ATT_HEADS, u % ATT_HEADS

        def gla_unit(s):
            u = jnp.clip(s - plan.attn_units, 0, plan.gla_units - 1)
            return u // GLA_HEADS, u % GLA_HEADS

        a_in, a_out = _attn_specs(mlayer, seq, attn_unit)
        g_in, g_out = _gla_specs(mlayer, seq, gla_unit)
        in_specs += a_in + g_in
        out_specs += [a_out, g_out]
        out_shape += [jax.ShapeDtypeStruct((mz.shape[0], D_ATT), jnp.bfloat16),
                      jax.ShapeDtypeStruct((mz.shape[0], D_GLA), jnp.bfloat16)]
        scratch += _gla_scratch(seq)
        args += [mz, mkt, mz, mz, bias, g_att, mz, mkt, mz, mz, mgat, walt, bcol, g_gla]
    return pl.pallas_call(
        functools.partial(_proj_kernel, plan),
        grid=(plan.nn_steps + plan.nt_steps,),
        in_specs=in_specs,
        out_specs=out_specs,
        out_shape=out_shape,
        scratch_shapes=scratch,
        compiler_params=pltpu.CompilerParams(
            dimension_semantics=("arbitrary",),
            vmem_limit_bytes=VMEM_LIMIT_FUSED if mix is not None else VMEM_LIMIT),
        name="proj_mix" if mix is not None else "proj",
    )(*args)


def _out_proj_kernel(fused, og_ref, oa_ref, wg_ref, wa_ref, x_ref, gpost_ref, *rest):
    y = (jnp.dot(og_ref[...], wg_ref[...], preferred_element_type=jnp.float32)
         + jnp.dot(oa_ref[...], wa_ref[...], preferred_element_type=jnp.float32))
    h = x_ref[...] + y * _rms_scale(y) * gpost_ref[...]
    if fused:
        gpre_ref, h_ref, hn_ref = rest
        hn_ref[...] = (h * _rms_scale(h) * gpre_ref[...]).astype(hn_ref.dtype)
    else:
        (h_ref,) = rest
    h_ref[...] = h


def _out_proj(layer, og, oa, w_out, x, x_row0, g_post, g_pre=None):
    m = og.shape[0]
    fused = g_pre is not None
    blk0 = x_row0 // OUT_TM
    in_specs = [
        pl.BlockSpec((OUT_TM, D_GLA), lambda i: (i, 0)),
        pl.BlockSpec((OUT_TM, D_ATT), lambda i: (i, 0)),
        pl.BlockSpec((None, D_GLA, D_MODEL), lambda i: (layer, 0, 0)),
        pl.BlockSpec((None, D_ATT, D_MODEL), lambda i: (layer, 1, 0)),
        pl.BlockSpec((OUT_TM, D_MODEL), lambda i: (blk0 + i, 0)),
        pl.BlockSpec((None, 1, D_MODEL), lambda i: (layer, 0, 0)),
    ]
    out_specs = [pl.BlockSpec((OUT_TM, D_MODEL), lambda i: (i, 0))]
    out_shape = [jax.ShapeDtypeStruct((m, D_MODEL), jnp.float32)]
    args = [og, oa, w_out, w_out, x, g_post]
    if fused:
        in_specs.append(pl.BlockSpec((None, 1, D_MODEL), lambda i: (layer + 1, 0, 0)))
        out_specs.append(pl.BlockSpec((OUT_TM, D_MODEL), lambda i: (i, 0)))
        out_shape.append(jax.ShapeDtypeStruct((m, D_MODEL), jnp.bfloat16))
        args.append(g_pre)
    return pl.pallas_call(
        functools.partial(_out_proj_kernel, fused),
        grid=(m // OUT_TM,),
        in_specs=in_specs,
        out_specs=out_specs,
        out_shape=out_shape,
        compiler_params=pltpu.CompilerParams(
            dimension_semantics=("parallel",), vmem_limit_bytes=VMEM_LIMIT),
        name="out_proj",
    )(*args)


def kernel(x, w_in, w_out, g_pre, g_post, w_alpha, b_alpha, g_gla, g_att, rel_bias):
    batch, seq, d_model = x.shape
    depth = w_in.shape[0]
    assert batch % N_STREAMS == 0
    sb = batch // N_STREAMS
    rows = sb * seq
    x2 = x.reshape(batch * seq, d_model)
    w_in_t = jnp.swapaxes(w_in, 1, 2).reshape(depth * D_IN, d_model)
    walt = jnp.pad(jnp.swapaxes(w_alpha, 1, 2), ((0, 0), (0, 0), (0, LANES - GATE_RANK))).astype(jnp.bfloat16)
    bcol = b_alpha[:, :, None]
    w_o = w_out.astype(jnp.bfloat16)
    bias = _band_bias_table(rel_bias)
    row = lambda p: p[:, None, :]
    mix_params = (walt, bcol, row(g_gla), bias, row(g_att))

    resid = [(x2, st * rows) for st in range(N_STREAMS)]
    hn = [_norm_call(0, x2, st * rows, rows, row(g_pre)) for st in range(N_STREAMS)]

    def finish_layer(l, st, og, oa):
        src, row0 = resid[st]
        if l + 1 < depth:
            h, hn[st] = _out_proj(l, og, oa, w_o, src, row0, row(g_post), row(g_pre))
        else:
            (h,) = _out_proj(l, og, oa, w_o, src, row0, row(g_post))
        resid[st] = (h, 0)

    pending = None
    for l in range(depth):
        for st in range(N_STREAMS):
            if pending is None:
                z, kt, gat = _proj(l, hn[st], w_in_t, sb, seq)
            else:
                pl_, pst, pz, pkt, pgat = pending
                z, kt, gat, oa, og = _proj(l, hn[st], w_in_t, sb, seq, mix=(pl_, pz, pkt, pgat) + mix_params)
                finish_layer(pl_, pst, og, oa)
            pending = (l, st, z, kt, gat)
    pl_, pst, pz, pkt, pgat = pending
    og = _gla(pl_, pz, pkt, pgat, walt, bcol, row(g_gla), sb, seq)
    oa = _band_attn(pl_, pz, pkt, bias, row(g_att), sb, seq)
    finish_layer(pl_, pst, og, oa)
    return jnp.concatenate([h for h, _ in resid], axis=0).reshape(batch, seq, d_model)
```

```python
import functools
import math

import jax
import jax.numpy as jnp
import numpy as np
from jax import lax
from jax.experimental import pallas as pl
from jax.experimental.pallas import tpu as pltpu

D_MODEL = 2048
CHUNK = 64
D_GLA = 1024
D_ATT = 1024
GLA_HEADS = 4
GLA_DK = 128
GLA_DV = 256
GLA_KW = GLA_HEADS * GLA_DK
GATE_RANK = 16
GLA_TAU = 16.0
ATT_HEADS = 8
ATT_HD = 128
LEFT_CHUNKS = 8
REL_CLIP = 128
EPS = 1e-6
LOG2E = math.log2(math.e)

LANES = 128

W_GA = 2 * GLA_KW + 2 * D_GLA
W_AQ = W_GA + GATE_RANK
D_IN = W_AQ + 4 * D_ATT
ROW_ALIGN = GATE_RANK
PROJ_TN = 512
Z_GQ, Z_GV, Z_GG = 0, GLA_KW, GLA_KW + D_GLA
Z_AQ, Z_AV, Z_AG = 0, D_ATT, 2 * D_ATT
KT_G, KT_A = 0, GLA_KW
D_KT = GLA_KW + D_ATT

PROJ_TM = 2048
NORM_TM = 512
OUT_TM = 256
GLA_GROUP = 256
GLA_UNROLL = 2
ATT_TQ = 128
ATT_WIN = ATT_TQ + LEFT_CHUNKS * CHUNK
CAST_ROWS = 256
VMEM_LIMIT = 48 * 1024 * 1024
VMEM_LIMIT_FUSED = 56 * 1024 * 1024


def _silu(x):
    half = 0.5 * x
    return half + half * jnp.tanh(half)


def _log_sigmoid(x):
    return jnp.minimum(x, 0.0) - jnp.log(1.0 + jnp.exp(-jnp.abs(x)))


def _rms_scale(x):
    return lax.rsqrt(jnp.mean(x * x, axis=-1, keepdims=True) + EPS)


def _norm_kernel(x_ref, gpre_ref, hn_ref):
    x = x_ref[...]
    hn_ref[...] = (x * _rms_scale(x) * gpre_ref[...]).astype(hn_ref.dtype)


def _norm_call(layer, x, g_pre):
    rows = x.shape[0]
    return pl.pallas_call(
        _norm_kernel,
        grid=(rows // NORM_TM,),
        in_specs=[pl.BlockSpec((NORM_TM, D_MODEL), lambda i: (i, 0)),
                  pl.BlockSpec((None, 1, D_MODEL), lambda i: (layer, 0, 0))],
        out_specs=pl.BlockSpec((NORM_TM, D_MODEL), lambda i: (i, 0)),
        out_shape=jax.ShapeDtypeStruct((rows, D_MODEL), jnp.bfloat16),
        compiler_params=pltpu.CompilerParams(
            dimension_semantics=("parallel",), vmem_limit_bytes=VMEM_LIMIT),
        name="norm",
    )(x, g_pre)


def _split3_bf16(x):
    hi = x.astype(jnp.bfloat16)
    r1 = x - hi.astype(jnp.float32)
    mid = r1.astype(jnp.bfloat16)
    lo = (r1 - mid.astype(jnp.float32)).astype(jnp.bfloat16)
    return hi, mid, lo


def _gla_unit(straight_line, q_ref, kt_ref, v_ref, gate_ref, gat_ref, walt_ref, bcol_ref, g_ref, o_ref,
              st_ref, parts_ref, sums_ref, u_ref, a_ref):
    seq = q_ref.shape[0]
    n_groups = seq // GLA_GROUP
    chunks_per_group = GLA_GROUP // CHUNK
    r = lax.broadcasted_iota(jnp.int32, (3 * GLA_GROUP, 2 * GLA_GROUP), 0) % GLA_GROUP
    c = lax.broadcasted_iota(jnp.int32, (3 * GLA_GROUP, 2 * GLA_GROUP), 1)
    sums = jnp.where((r // CHUNK == (c % GLA_GROUP) // CHUNK) & (r <= c), 1.0, 0.0).astype(jnp.bfloat16)
    frame_chunk = lax.broadcasted_iota(jnp.int32, (GLA_DK, GLA_GROUP), 1) // CHUNK
    ga_pad = jnp.zeros((LANES - GATE_RANK, GLA_GROUP), jnp.bfloat16)
    g = g_ref[...]

    def aligned(start, size):
        return pl.ds(start if isinstance(start, int) else pl.multiple_of(start, size), size)

    def gate_step(gi, carry):
        ga = jnp.concatenate([gat_ref[:, aligned(gi * GLA_GROUP, GLA_GROUP)], ga_pad], axis=0)
        pre = jnp.dot(walt_ref[...], ga, preferred_element_type=jnp.float32) + bcol_ref[...]
        log_a = _log_sigmoid(pre) * (1.0 / GLA_TAU)
        parts_ref[aligned(gi * GLA_DK, GLA_DK), :] = jnp.concatenate(_split3_bf16(log_a), axis=1)
        return carry

    def decay_step(gi, carry):
        cols = aligned(gi * GLA_GROUP, GLA_GROUP)
        both = sums_ref[aligned(gi * GLA_DK, GLA_DK), :]
        run, total = both[:, :GLA_GROUP], both[:, GLA_GROUP:]
        k_dec = (kt_ref[:, cols].astype(jnp.float32) * jnp.exp(total - run)).astype(jnp.bfloat16)
        k_own = jnp.concatenate([jnp.where(frame_chunk == ci, k_dec, jnp.zeros_like(k_dec))
                                 for ci in range(chunks_per_group)], axis=0)
        u = jnp.dot(k_own, v_ref[cols, :], preferred_element_type=jnp.float32)
        a = jnp.exp(total)
        for ci in range(chunks_per_group):
            u_ref[gi, ci] = u[ci * GLA_DK:(ci + 1) * GLA_DK]
            a_ref[gi, ci] = jnp.broadcast_to(a[:, ci * CHUNK:ci * CHUNK + 1], (GLA_DK, GLA_DV))
        return carry

    def state_step(gi, carry):
        s_c = st_ref[...]
        for ci in range(chunks_per_group):
            s_c = a_ref[gi, ci] * s_c + u_ref[gi, ci]
            rows = aligned(gi * GLA_GROUP + ci * CHUNK, CHUNK)
            o = jnp.dot(q_ref[rows, :], s_c.astype(jnp.bfloat16), preferred_element_type=jnp.float32)
            o = o * _rms_scale(o) * g
            o_ref[rows, :] = (o * _silu(gate_ref[rows, :].astype(jnp.float32))).astype(o_ref.dtype)
        st_ref[...] = s_c
        return carry

    def run_pass(step, unroll):
        if straight_line:
            for gi in range(n_groups):
                step(gi, 0)
        else:
            lax.fori_loop(0, n_groups, step, 0, unroll=unroll)

    st_ref[...] = jnp.zeros_like(st_ref)
    run_pass(gate_step, 2 * GLA_UNROLL)
    sums_ref[...] = jnp.dot(parts_ref[...], sums, preferred_element_type=jnp.float32)
    run_pass(decay_step, 2 * GLA_UNROLL)
    run_pass(state_step, GLA_UNROLL)


def _gla_specs(layer, seq, unit):
    b_h = lambda off: (lambda *idx: (unit(*idx)[0], off + unit(*idx)[1]))
    in_specs = [
        pl.BlockSpec((seq, GLA_DK), b_h(Z_GQ // GLA_DK)),
        pl.BlockSpec((GLA_DK, seq), lambda *idx: (KT_G // GLA_DK + unit(*idx)[1], unit(*idx)[0])),
        pl.BlockSpec((seq, GLA_DV), b_h(Z_GV // GLA_DV)),
        pl.BlockSpec((seq, GLA_DV), b_h(Z_GG // GLA_DV)),
        pl.BlockSpec((None, GATE_RANK, seq), lambda *idx: (0, 0, unit(*idx)[0])),
        pl.BlockSpec((None, GLA_DK, LANES), lambda *idx: (layer, unit(*idx)[1], 0)),
        pl.BlockSpec((None, GLA_DK, 1), lambda *idx: (layer, unit(*idx)[1], 0)),
        pl.BlockSpec((None, 1, GLA_DV), lambda *idx: (layer, 0, unit(*idx)[1])),
    ]
    out_spec = pl.BlockSpec((seq, GLA_DV), lambda *idx: unit(*idx))
    return in_specs, out_spec


def _gla_scratch(seq):
    n_groups = seq // GLA_GROUP
    chunks_per_group = GLA_GROUP // CHUNK
    return [pltpu.VMEM((GLA_DK, GLA_DV), jnp.float32),
            pltpu.VMEM((n_groups * GLA_DK, 3 * GLA_GROUP), jnp.bfloat16),
            pltpu.VMEM((n_groups * GLA_DK, 2 * GLA_GROUP), jnp.float32),
            pltpu.VMEM((n_groups, chunks_per_group, GLA_DK, GLA_DV), jnp.float32),
            pltpu.VMEM((n_groups, chunks_per_group, GLA_DK, GLA_DV), jnp.float32)]


def _attn_scores(q_ref, kt_ref, bias_ref, q_start, k_start, width):
    s = jnp.dot(q_ref[q_start:q_start + ATT_TQ, :], kt_ref[:, k_start:k_start + width],
                preferred_element_type=jnp.float32)
    return s + bias_ref[0, :, ATT_WIN - width:]


def _attn_softmax(s2):
    m = jnp.max(s2, axis=-1, keepdims=True)
    e = jnp.exp2(s2 - m)
    return e.astype(jnp.bfloat16), jnp.sum(e, axis=-1, keepdims=True)


def _attn_finish(p, denom, v_ref, gate_ref, g, o_ref, q_start, k_start, width):
    o = jnp.dot(p, v_ref[k_start:k_start + width, :], preferred_element_type=jnp.float32)
    o = o * (1.0 / denom)
    o = o * _rms_scale(o) * g
    rows = slice(q_start, q_start + ATT_TQ)
    o_ref[rows, :] = (o * _silu(gate_ref[rows, :].astype(jnp.float32))).astype(o_ref.dtype)


def _attn_unit(q_ref, kt_ref, v_ref, gate_ref, bias_ref, g_ref, o_ref):
    seq = q_ref.shape[0]
    g = g_ref[...]
    n_tiles = seq // ATT_TQ

    def window(t):
        k_start = max(0, (t + 1) * ATT_TQ - ATT_WIN)
        return t * ATT_TQ, k_start, (t + 1) * ATT_TQ - k_start

    scores, probs = {}, {}
    for step in range(n_tiles + 2):
        if step < n_tiles:
            scores[step] = _attn_scores(q_ref, kt_ref, bias_ref, *window(step))
        if 0 <= step - 1 < n_tiles:
            probs[step - 1] = _attn_softmax(scores.pop(step - 1))
        if 0 <= step - 2:
            _attn_finish(*probs.pop(step - 2), v_ref, gate_ref, g, o_ref, *window(step - 2))


def _attn_specs(layer, seq, unit):
    blk = lambda off: off // ATT_HD
    b_h = lambda off: (lambda *idx: (unit(*idx)[0], off + unit(*idx)[1]))
    in_specs = [
        pl.BlockSpec((seq, ATT_HD), b_h(blk(Z_AQ))),
        pl.BlockSpec((ATT_HD, seq), lambda *idx: (blk(KT_A) + unit(*idx)[1], unit(*idx)[0])),
        pl.BlockSpec((seq, ATT_HD), b_h(blk(Z_AV))),
        pl.BlockSpec((seq, ATT_HD), b_h(blk(Z_AG))),
        pl.BlockSpec((None, 1, ATT_TQ, ATT_WIN), lambda *idx: (layer, unit(*idx)[1], 0, 0)),
        pl.BlockSpec((None, 1, ATT_HD), lambda *idx: (layer, 0, unit(*idx)[1])),
    ]
    out_spec = pl.BlockSpec((seq, ATT_HD), lambda *idx: unit(*idx))
    return in_specs, out_spec


def _band_attn(layer, z, kt, bias, g_att, batch, seq):
    in_specs, out_spec = _attn_specs(layer, seq, lambda b, h: (b, h))
    return pl.pallas_call(
        _attn_unit,
        grid=(batch, ATT_HEADS),
        in_specs=in_specs,
        out_specs=out_spec,
        out_shape=jax.ShapeDtypeStruct((z.shape[0], D_ATT), jnp.bfloat16),
        compiler_params=pltpu.CompilerParams(
            dimension_semantics=("parallel", "parallel"), vmem_limit_bytes=VMEM_LIMIT),
        name="band_attn",
    )(z, kt, z, z, bias, g_att)


def _band_bias_table(rel_bias):
    lead = rel_bias.shape[:-1]
    span = ATT_TQ + ATT_WIN - 1
    n_far = ATT_WIN - REL_CLIP
    rb = rel_bias.astype(jnp.float32) * LOG2E
    c = jnp.concatenate([jnp.broadcast_to(rb[..., 2 * REL_CLIP:], lead + (n_far,)),
                         rb[..., 2 * REL_CLIP - 1:0:-1]], axis=-1)
    cp = jnp.pad(c, [(0, 0)] * len(lead) + [(0, 1)])
    skew = jnp.tile(cp, ATT_TQ)[..., :ATT_TQ * span].reshape(lead + (ATT_TQ, span))
    table = skew[..., ATT_TQ - 1:]
    i = np.arange(ATT_TQ)[:, None]
    j = np.arange(ATT_WIN)[None, :]
    first = (i // CHUNK) * CHUNK
    in_band = (j >= first) & (j < first + (LEFT_CHUNKS + 1) * CHUNK)
    return jnp.where(in_band, table, -1e30)


_GLA_ROWS = [0] + [2 * GLA_KW + t * PROJ_TN for t in range(2 * D_GLA // PROJ_TN)]
_ATT_ROWS = ([W_AQ + t * PROJ_TN for t in range(D_ATT // PROJ_TN)]
             + [W_AQ + 2 * D_ATT + t * PROJ_TN for t in range(2 * D_ATT // PROJ_TN)])
_NT_ROWS = [GLA_KW] + [W_AQ + D_ATT + t * PROJ_TN for t in range(D_ATT // PROJ_TN)]


def _table_lookup(table, j):
    val = table[0] + j * 0
    for t in range(1, len(table)):
        val = jnp.where(j >= t, table[t], val)
    return val


def _cast_rows(src_ref, dst_ref, n_rows):
    def body(r, carry):
        rows = pl.ds(pl.multiple_of(r * CAST_ROWS, CAST_ROWS), CAST_ROWS)
        dst_ref[rows, :] = src_ref[rows, :].astype(dst_ref.dtype)
        return carry
    lax.fori_loop(0, n_rows // CAST_ROWS, body, 0)


def _w_rows_spec(layer, table, n_rows, tiles):
    units = [(layer * D_IN + r) // ROW_ALIGN for r in table]
    return pl.BlockSpec((pl.Element(n_rows), pl.Element(D_MODEL)),
                        lambda s: (_table_lookup(units, s // tiles) * ROW_ALIGN, 0))


def _proj_nn_kernel(tiles, n_units, hn_ref, wt_ref, *refs):
    if n_units:
        gla_in, (z_ref, og_ref, wbf_ref), gla_scratch = refs[:8], refs[8:11], refs[11:]
    else:
        z_ref, wbf_ref = refs
    s = pl.program_id(0)

    @pl.when(s % tiles == 0)
    def _():
        _cast_rows(wt_ref, wbf_ref, PROJ_TN)

    def matmul():
        z_ref[...] = lax.dot_general(hn_ref[...], wbf_ref[...], (((1,), (1,)), ((), ())),
                                     preferred_element_type=jnp.float32).astype(z_ref.dtype)

    if n_units:
        @pl.when(s < n_units)
        def _():
            matmul()
            _gla_unit(True, *gla_in, og_ref, *gla_scratch)

        pl.when(s >= n_units)(matmul)
    else:
        matmul()


def _proj_nn(layer, hn, w_in_t, table, gla=None):
    m = hn.shape[0]
    tiles = m // PROJ_TM
    steps = len(table) * tiles
    in_specs = [pl.BlockSpec((PROJ_TM, D_MODEL), lambda s: (s % tiles, 0)),
                _w_rows_spec(layer, table, PROJ_TN, tiles)]
    out_specs = [pl.BlockSpec((PROJ_TM, PROJ_TN), lambda s: (s % tiles, s // tiles))]
    out_shape = [jax.ShapeDtypeStruct((m, len(table) * PROJ_TN), jnp.bfloat16)]
    scratch = [pltpu.VMEM((PROJ_TN, D_MODEL), jnp.bfloat16)]
    args = [hn, w_in_t]
    n_units = 0
    if gla is not None:
        zg, kt, gat, walt, bcol, g_gla, batch, seq = gla
        n_units = batch * GLA_HEADS
        assert n_units <= steps

        def unit(s):
            u = jnp.minimum(s, n_units - 1)
            return u // GLA_HEADS, u % GLA_HEADS

        g_in, g_out = _gla_specs(layer, seq, unit)
        in_specs += g_in
        out_specs.append(g_out)
        out_shape.append(jax.ShapeDtypeStruct((m, D_GLA), jnp.bfloat16))
        scratch += _gla_scratch(seq)
        args += [zg, kt, zg, zg, gat, walt, bcol, g_gla]
    return pl.pallas_call(
        functools.partial(_proj_nn_kernel, tiles, n_units),
        grid=(steps,),
        in_specs=in_specs,
        out_specs=out_specs,
        out_shape=out_shape,
        scratch_shapes=scratch,
        compiler_params=pltpu.CompilerParams(
            dimension_semantics=("arbitrary",),
            vmem_limit_bytes=VMEM_LIMIT_FUSED if n_units else VMEM_LIMIT),
        name="proj_att_gla" if n_units else "proj_gla",
    )(*args)


def _proj_nt_kernel(tiles, hn_ref, wt_ref, wga_ref, kt_ref, gat_ref, wbf_ref):
    s = pl.program_id(0)

    @pl.when(s % tiles == 0)
    def _():
        _cast_rows(wt_ref, wbf_ref, PROJ_TN)

    @pl.when(s == 0)
    def _():
        wbf_ref[PROJ_TN:, :] = wga_ref[...].astype(wbf_ref.dtype)

    res = lax.dot_general(wbf_ref[...], hn_ref[...], (((1,), (1,)), ((), ())),
                          preferred_element_type=jnp.float32)
    scale = jnp.where(s < tiles, GLA_DK ** -0.5, ATT_HD ** -0.5 * LOG2E)
    kt_ref[...] = (res[:PROJ_TN] * scale).astype(kt_ref.dtype)
    gat_ref[...] = res[PROJ_TN:].astype(gat_ref.dtype)


def _proj_nt(layer, hn, w_in_t):
    m = hn.shape[0]
    tiles = m // PROJ_TM
    return pl.pallas_call(
        functools.partial(_proj_nt_kernel, tiles),
        grid=(len(_NT_ROWS) * tiles,),
        in_specs=[pl.BlockSpec((PROJ_TM, D_MODEL), lambda s: (s % tiles, 0)),
                  _w_rows_spec(layer, _NT_ROWS, PROJ_TN, tiles),
                  _w_rows_spec(layer, [W_GA], GATE_RANK, len(_NT_ROWS) * tiles)],
        out_specs=[pl.BlockSpec((PROJ_TN, PROJ_TM), lambda s: (s // tiles, s % tiles)),
                   pl.BlockSpec((None, GATE_RANK, PROJ_TM), lambda s: (s // tiles, 0, s % tiles))],
        out_shape=[jax.ShapeDtypeStruct((D_KT, m), jnp.bfloat16),
                   jax.ShapeDtypeStruct((len(_NT_ROWS), GATE_RANK, m), jnp.bfloat16)],
        scratch_shapes=[pltpu.VMEM((PROJ_TN + GATE_RANK, D_MODEL), jnp.bfloat16)],
        compiler_params=pltpu.CompilerParams(
            dimension_semantics=("arbitrary",), vmem_limit_bytes=VMEM_LIMIT),
        name="proj_nt",
    )(hn, w_in_t, w_in_t)


def _out_proj_kernel(fused, og_ref, oa_ref, w_ref, x_ref, gpost_ref, *rest):
    if fused:
        gpre_ref, h_ref, hn_ref, wbf_ref = rest
    else:
        h_ref, wbf_ref = rest

    @pl.when(pl.program_id(0) == 0)
    def _():
        _cast_rows(w_ref, wbf_ref, D_GLA + D_ATT)

    y = (jnp.dot(og_ref[...], wbf_ref[:D_GLA, :], preferred_element_type=jnp.float32)
         + jnp.dot(oa_ref[...], wbf_ref[D_GLA:, :], preferred_element_type=jnp.float32))
    h = x_ref[...] + y * _rms_scale(y) * gpost_ref[...]
    if fused:
        hn_ref[...] = (h * _rms_scale(h) * gpre_ref[...]).astype(hn_ref.dtype)
    h_ref[...] = h


def _out_proj(layer, og, oa, w_out, x, g_post, g_pre=None):
    m = x.shape[0]
    fused = g_pre is not None
    in_specs = [
        pl.BlockSpec((OUT_TM, D_GLA), lambda i: (i, 0)),
        pl.BlockSpec((OUT_TM, D_ATT), lambda i: (i, 0)),
        pl.BlockSpec((None, D_GLA + D_ATT, D_MODEL), lambda i: (layer, 0, 0), pipeline_mode=pl.Buffered(1)),
        pl.BlockSpec((OUT_TM, D_MODEL), lambda i: (i, 0)),
        pl.BlockSpec((None, 1, D_MODEL), lambda i: (layer, 0, 0)),
    ]
    out_specs = [pl.BlockSpec((OUT_TM, D_MODEL), lambda i: (i, 0))]
    out_shape = [jax.ShapeDtypeStruct((m, D_MODEL), jnp.float32)]
    args = [og, oa, w_out, x, g_post]
    if fused:
        in_specs.append(pl.BlockSpec((None, 1, D_MODEL), lambda i: (layer + 1, 0, 0)))
        out_specs.append(pl.BlockSpec((OUT_TM, D_MODEL), lambda i: (i, 0)))
        out_shape.append(jax.ShapeDtypeStruct((m, D_MODEL), jnp.bfloat16))
        args.append(g_pre)
    return pl.pallas_call(
        functools.partial(_out_proj_kernel, fused),
        grid=(m // OUT_TM,),
        in_specs=in_specs,
        out_specs=out_specs,
        out_shape=out_shape,
        scratch_shapes=[pltpu.VMEM((D_GLA + D_ATT, D_MODEL), jnp.bfloat16)],
        compiler_params=pltpu.CompilerParams(
            dimension_semantics=("arbitrary",), vmem_limit_bytes=VMEM_LIMIT),
        name="out_proj",
    )(*args)


def kernel(x, w_in, w_out, g_pre, g_post, w_alpha, b_alpha, g_gla, g_att, rel_bias):
    batch, seq, d_model = x.shape
    depth = w_in.shape[0]
    h = x.reshape(batch * seq, d_model)
    w_in_t = jnp.swapaxes(w_in, 1, 2).reshape(depth * D_IN, d_model)
    walt = jnp.pad(jnp.swapaxes(w_alpha, 1, 2), ((0, 0), (0, 0), (0, LANES - GATE_RANK))).astype(jnp.bfloat16)
    bcol = b_alpha[:, :, None]
    bias = _band_bias_table(rel_bias)
    row = lambda p: p[:, None, :]
    hn = _norm_call(0, h, row(g_pre))
    for l in range(depth):
        kt, gat = _proj_nt(l, hn, w_in_t)
        (zg,) = _proj_nn(l, hn, w_in_t, _GLA_ROWS)
        za, og = _proj_nn(l, hn, w_in_t, _ATT_ROWS, gla=(zg, kt, gat, walt, bcol, row(g_gla), batch, seq))
        oa = _band_attn(l, za, kt, bias, row(g_att), batch, seq)
        if l + 1 < depth:
            h, hn = _out_proj(l, og, oa, w_out, h, row(g_post), row(g_pre))
        else:
            (h,) = _out_proj(l, og, oa, w_out, h, row(g_post))
    return h.reshape(batch, seq, d_model)
```

```python
import functools
import math

import jax
import jax.numpy as jnp
import numpy as np
from jax import lax
from jax.experimental import pallas as pl
from jax.experimental.pallas import tpu as pltpu

D_MODEL = 2048
CHUNK = 64
D_GLA = 1024
D_ATT = 1024
GLA_HEADS = 4
GLA_DK = 128
GLA_DV = 256
GLA_KW = GLA_HEADS * GLA_DK
GATE_RANK = 16
GLA_TAU = 16.0
ATT_HEADS = 8
ATT_HD = 128
LEFT_CHUNKS = 8
REL_CLIP = 128
EPS = 1e-6
LOG2E = math.log2(math.e)

LANES = 128

W_GA = 2 * GLA_KW + 2 * D_GLA
W_AQ = W_GA + GATE_RANK
D_IN = W_AQ + 4 * D_ATT
ROW_ALIGN = GATE_RANK
PROJ_TN = 512
Z_GQ, Z_GV, Z_GG = 0, GLA_KW, GLA_KW + D_GLA
Z_AQ, Z_AV, Z_AG = 0, D_ATT, 2 * D_ATT
KT_G, KT_A = 0, GLA_KW
D_KT = GLA_KW + D_ATT

PROJ_TM = 2048
MIX_PIECE_ROWS, MIX_PIECE_COLS = 256, 256
NORM_TM = 512
OUT_TM = 256
GLA_GROUP = 256
GLA_UNROLL = 2
ATT_TQ = 128
ATT_WIN = ATT_TQ + LEFT_CHUNKS * CHUNK
CAST_ROWS = 256
VMEM_LIMIT = 48 * 1024 * 1024
VMEM_LIMIT_FUSED = 56 * 1024 * 1024


def _silu(x):
    half = 0.5 * x
    return half + half * jnp.tanh(half)


def _log_sigmoid(x):
    return jnp.minimum(x, 0.0) - jnp.log(1.0 + jnp.exp(-jnp.abs(x)))


def _rms_scale(x):
    return lax.rsqrt(jnp.mean(x * x, axis=-1, keepdims=True) + EPS)


def _norm_kernel(x_ref, gpre_ref, hn_ref):
    x = x_ref[...]
    hn_ref[...] = (x * _rms_scale(x) * gpre_ref[...]).astype(hn_ref.dtype)


def _norm_call(layer, x, g_pre):
    rows = x.shape[0]
    return pl.pallas_call(
        _norm_kernel,
        grid=(rows // NORM_TM,),
        in_specs=[pl.BlockSpec((NORM_TM, D_MODEL), lambda i: (i, 0)),
                  pl.BlockSpec((None, 1, D_MODEL), lambda i: (layer, 0, 0))],
        out_specs=pl.BlockSpec((NORM_TM, D_MODEL), lambda i: (i, 0)),
        out_shape=jax.ShapeDtypeStruct((rows, D_MODEL), jnp.bfloat16),
        compiler_params=pltpu.CompilerParams(
            dimension_semantics=("parallel",), vmem_limit_bytes=VMEM_LIMIT),
        name="norm",
    )(x, g_pre)


def _split3_bf16(x):
    hi = x.astype(jnp.bfloat16)
    r1 = x - hi.astype(jnp.float32)
    mid = r1.astype(jnp.bfloat16)
    lo = (r1 - mid.astype(jnp.float32)).astype(jnp.bfloat16)
    return hi, mid, lo


def _gla_unit(side_work, q_ref, kt_ref, v_ref, gate_ref, gat_ref, walt_ref, bcol_ref, g_ref, o_ref,
              st_ref, parts_ref, sums_ref, u_ref, a_ref):
    seq = q_ref.shape[0]
    n_groups = seq // GLA_GROUP
    chunks_per_group = GLA_GROUP // CHUNK
    r = lax.broadcasted_iota(jnp.int32, (3 * GLA_GROUP, 2 * GLA_GROUP), 0) % GLA_GROUP
    c = lax.broadcasted_iota(jnp.int32, (3 * GLA_GROUP, 2 * GLA_GROUP), 1)
    sums = jnp.where((r // CHUNK == (c % GLA_GROUP) // CHUNK) & (r <= c), 1.0, 0.0).astype(jnp.bfloat16)
    frame_chunk = lax.broadcasted_iota(jnp.int32, (GLA_DK, GLA_GROUP), 1) // CHUNK
    ga_pad = jnp.zeros((LANES - GATE_RANK, GLA_GROUP), jnp.bfloat16)
    g = g_ref[...]

    def aligned(start, size):
        return pl.ds(start if isinstance(start, int) else pl.multiple_of(start, size), size)

    def gate_step(gi, carry):
        ga = jnp.concatenate([gat_ref[:, aligned(gi * GLA_GROUP, GLA_GROUP)], ga_pad], axis=0)
        pre = jnp.dot(walt_ref[...], ga, preferred_element_type=jnp.float32) + bcol_ref[...]
        log_a = _log_sigmoid(pre) * (1.0 / GLA_TAU)
        parts_ref[aligned(gi * GLA_DK, GLA_DK), :] = jnp.concatenate(_split3_bf16(log_a), axis=1)
        return carry

    def decay_step(gi, carry):
        cols = aligned(gi * GLA_GROUP, GLA_GROUP)
        both = sums_ref[aligned(gi * GLA_DK, GLA_DK), :]
        run, total = both[:, :GLA_GROUP], both[:, GLA_GROUP:]
        k_dec = (kt_ref[:, cols].astype(jnp.float32) * jnp.exp(total - run)).astype(jnp.bfloat16)
        k_own = jnp.concatenate([jnp.where(frame_chunk == ci, k_dec, jnp.zeros_like(k_dec))
                                 for ci in range(chunks_per_group)], axis=0)
        u = jnp.dot(k_own, v_ref[cols, :], preferred_element_type=jnp.float32)
        a = jnp.exp(total)
        for ci in range(chunks_per_group):
            u_ref[gi, ci] = u[ci * GLA_DK:(ci + 1) * GLA_DK]
            a_ref[gi, ci] = jnp.broadcast_to(a[:, ci * CHUNK:ci * CHUNK + 1], (GLA_DK, GLA_DV))
        return carry

    def state_step(gi, carry):
        s_c = st_ref[...]
        for ci in range(chunks_per_group):
            s_c = a_ref[gi, ci] * s_c + u_ref[gi, ci]
            rows = aligned(gi * GLA_GROUP + ci * CHUNK, CHUNK)
            o = jnp.dot(q_ref[rows, :], s_c.astype(jnp.bfloat16), preferred_element_type=jnp.float32)
            o = o * _rms_scale(o) * g
            o_ref[rows, :] = (o * _silu(gate_ref[rows, :].astype(jnp.float32))).astype(o_ref.dtype)
        st_ref[...] = s_c
        return carry

    def sums_step():
        sums_ref[...] = jnp.dot(parts_ref[...], sums, preferred_element_type=jnp.float32)

    stages = ([functools.partial(gate_step, gi, 0) for gi in range(n_groups)] + [sums_step]
              + [functools.partial(decay_step, gi, 0) for gi in range(n_groups)]
              + [functools.partial(state_step, gi, 0) for gi in range(n_groups)])
    side_at = {(k * len(stages)) // len(side_work): fn for k, fn in enumerate(side_work)}
    st_ref[...] = jnp.zeros_like(st_ref)
    for k, stage in enumerate(stages):
        if k in side_at:
            side_at[k]()
        stage()


def _gla_specs(layer, seq, unit):
    b_h = lambda off: (lambda *idx: (unit(*idx)[0], off + unit(*idx)[1]))
    in_specs = [
        pl.BlockSpec((seq, GLA_DK), b_h(Z_GQ // GLA_DK)),
        pl.BlockSpec((GLA_DK, seq), lambda *idx: (KT_G // GLA_DK + unit(*idx)[1], unit(*idx)[0])),
        pl.BlockSpec((seq, GLA_DV), b_h(Z_GV // GLA_DV)),
        pl.BlockSpec((seq, GLA_DV), b_h(Z_GG // GLA_DV)),
        pl.BlockSpec((None, GATE_RANK, seq), lambda *idx: (0, 0, unit(*idx)[0])),
        pl.BlockSpec((None, GLA_DK, LANES), lambda *idx: (layer, unit(*idx)[1], 0)),
        pl.BlockSpec((None, GLA_DK, 1), lambda *idx: (layer, unit(*idx)[1], 0)),
        pl.BlockSpec((None, 1, GLA_DV), lambda *idx: (layer, 0, unit(*idx)[1])),
    ]
    out_spec = pl.BlockSpec((seq, GLA_DV), lambda *idx: unit(*idx))
    return in_specs, out_spec


def _gla_scratch(seq):
    n_groups = seq // GLA_GROUP
    chunks_per_group = GLA_GROUP // CHUNK
    return [pltpu.VMEM((GLA_DK, GLA_DV), jnp.float32),
            pltpu.VMEM((n_groups * GLA_DK, 3 * GLA_GROUP), jnp.bfloat16),
            pltpu.VMEM((n_groups * GLA_DK, 2 * GLA_GROUP), jnp.float32),
            pltpu.VMEM((n_groups, chunks_per_group, GLA_DK, GLA_DV), jnp.float32),
            pltpu.VMEM((n_groups, chunks_per_group, GLA_DK, GLA_DV), jnp.float32)]


def _attn_scores(q_ref, kt_ref, bias_ref, q_start, k_start, width):
    s = jnp.dot(q_ref[q_start:q_start + ATT_TQ, :], kt_ref[:, k_start:k_start + width],
                preferred_element_type=jnp.float32)
    return s + bias_ref[:, ATT_WIN - width:]


def _attn_softmax(s2):
    m = jnp.max(s2, axis=-1, keepdims=True)
    e = jnp.exp2(s2 - m)
    return e.astype(jnp.bfloat16), jnp.sum(e, axis=-1, keepdims=True)


def _attn_finish(p, denom, v_ref, gate_ref, g, o_ref, q_start, k_start, width):
    o = jnp.dot(p, v_ref[k_start:k_start + width, :], preferred_element_type=jnp.float32)
    o = o * (1.0 / denom)
    o = o * _rms_scale(o) * g
    rows = slice(q_start, q_start + ATT_TQ)
    o_ref[rows, :] = (o * _silu(gate_ref[rows, :].astype(jnp.float32))).astype(o_ref.dtype)


def _attn_kernel(q_ref, kt_ref, v_ref, gate_ref, brow_ref, mask_ref, g_ref, o_ref, bias_ref):
    seq = q_ref.shape[0]
    g = g_ref[...]
    n_tiles = seq // ATT_TQ

    @pl.when(pl.program_id(1) == 0)
    def _():
        rows = jnp.broadcast_to(brow_ref[...], (ATT_TQ, ATT_TQ + ATT_WIN))
        bias_ref[...] = pltpu.roll(rows, 0, 1, stride=1, stride_axis=0)[:, :ATT_WIN] + mask_ref[...]

    def window(t):
        k_start = max(0, (t + 1) * ATT_TQ - ATT_WIN)
        return t * ATT_TQ, k_start, (t + 1) * ATT_TQ - k_start

    scores, probs = {}, {}
    for step in range(n_tiles + 2):
        if step < n_tiles:
            scores[step] = _attn_scores(q_ref, kt_ref, bias_ref, *window(step))
        if 0 <= step - 1 < n_tiles:
            probs[step - 1] = _attn_softmax(scores.pop(step - 1))
        if 0 <= step - 2:
            _attn_finish(*probs.pop(step - 2), v_ref, gate_ref, g, o_ref, *window(step - 2))


def _band_attn(layer, z, kt, bias_rows, band_mask, g_att, batch, seq):
    blk = lambda off: off // ATT_HD
    b_h = lambda off: (lambda h, b: (b, off + h))
    return pl.pallas_call(
        _attn_kernel,
        grid=(ATT_HEADS, batch),
        in_specs=[
            pl.BlockSpec((seq, ATT_HD), b_h(blk(Z_AQ))),
            pl.BlockSpec((ATT_HD, seq), lambda h, b: (blk(KT_A) + h, b)),
            pl.BlockSpec((seq, ATT_HD), b_h(blk(Z_AV))),
            pl.BlockSpec((seq, ATT_HD), b_h(blk(Z_AG))),
            pl.BlockSpec((None, None, 1, ATT_TQ + ATT_WIN), lambda h, b: (layer, h, 0, 0)),
            pl.BlockSpec((ATT_TQ, ATT_WIN), lambda h, b: (0, 0)),
            pl.BlockSpec((None, 1, ATT_HD), lambda h, b: (layer, 0, h)),
        ],
        out_specs=pl.BlockSpec((seq, ATT_HD), lambda h, b: (b, h)),
        out_shape=jax.ShapeDtypeStruct((z.shape[0], D_ATT), jnp.bfloat16),
        scratch_shapes=[pltpu.VMEM((ATT_TQ, ATT_WIN), jnp.float32)],
        compiler_params=pltpu.CompilerParams(
            dimension_semantics=("arbitrary", "arbitrary"), vmem_limit_bytes=VMEM_LIMIT),
        name="band_attn",
    )(z, kt, z, z, bias_rows, band_mask, g_att)


def _band_bias_rows(rel_bias):
    lead = rel_bias.shape[:-1]
    rb = rel_bias.astype(jnp.float32) * LOG2E
    far = rb[..., 2 * REL_CLIP:]
    row = jnp.concatenate([jnp.broadcast_to(far, lead + (ATT_WIN - REL_CLIP - (ATT_TQ - 1),)),
                           rb[..., 2 * REL_CLIP - 1:0:-1],
                           jnp.zeros(lead + (1,), jnp.float32),
                           jnp.broadcast_to(far, lead + (ATT_TQ - 1,))], axis=-1)
    return row[..., None, :]


def _band_mask():
    i = np.arange(ATT_TQ)[:, None]
    j = np.arange(ATT_WIN)[None, :]
    first = (i // CHUNK) * CHUNK
    in_band = (j >= first) & (j < first + (LEFT_CHUNKS + 1) * CHUNK)
    return jnp.asarray(np.where(in_band, 0.0, -1e30), jnp.float32)


_GLA_ROWS = [0] + [2 * GLA_KW + t * PROJ_TN for t in range(2 * D_GLA // PROJ_TN)]
_ATT_ROWS = ([W_AQ + t * PROJ_TN for t in range(D_ATT // PROJ_TN)]
             + [W_AQ + 2 * D_ATT + t * PROJ_TN for t in range(2 * D_ATT // PROJ_TN)])
_NT_ROWS = [GLA_KW] + [W_AQ + D_ATT + t * PROJ_TN for t in range(D_ATT // PROJ_TN)]


def _table_lookup(table, j):
    val = table[0] + j * 0
    for t in range(1, len(table)):
        val = jnp.where(j >= t, table[t], val)
    return val


def _cast_rows(src_ref, dst_ref, n_rows):
    def body(r, carry):
        rows = pl.ds(pl.multiple_of(r * CAST_ROWS, CAST_ROWS), CAST_ROWS)
        dst_ref[rows, :] = src_ref[rows, :].astype(dst_ref.dtype)
        return carry
    lax.fori_loop(0, n_rows // CAST_ROWS, body, 0)


def _w_rows_spec(layer, table, n_rows, tiles):
    units = [(layer * D_IN + r) // ROW_ALIGN for r in table]
    return pl.BlockSpec((pl.Element(n_rows), pl.Element(D_MODEL)),
                        lambda s: (_table_lookup(units, s // tiles) * ROW_ALIGN, 0))


def _proj_nn_kernel(tiles, n_units, hn_ref, wt_ref, *refs):
    if n_units:
        gla_in, (z_ref, og_ref, wbf_ref), gla_scratch = refs[:8], refs[8:11], refs[11:]
    else:
        z_ref, wbf_ref = refs
    s = pl.program_id(0)

    @pl.when(s % tiles == 0)
    def _():
        _cast_rows(wt_ref, wbf_ref, PROJ_TN)

    def matmul_piece(r0, n_rows, c0, n_cols):
        def run():
            rows, cols = slice(r0, r0 + n_rows), slice(c0, c0 + n_cols)
            z_ref[rows, cols] = lax.dot_general(hn_ref[rows, :], wbf_ref[cols, :], (((1,), (1,)), ((), ())),
                                                preferred_element_type=jnp.float32).astype(z_ref.dtype)
        return run

    matmul = matmul_piece(0, PROJ_TM, 0, PROJ_TN)

    if n_units:
        @pl.when(s < n_units)
        def _():
            pieces = [matmul_piece(r0, MIX_PIECE_ROWS, c0, MIX_PIECE_COLS)
                      for r0 in range(0, PROJ_TM, MIX_PIECE_ROWS) for c0 in range(0, PROJ_TN, MIX_PIECE_COLS)]
            _gla_unit(pieces, *gla_in, og_ref, *gla_scratch)

        pl.when(s >= n_units)(matmul)
    else:
        matmul()


def _proj_nn(layer, hn, w_in_t, table, gla=None):
    m = hn.shape[0]
    tiles = m // PROJ_TM
    steps = len(table) * tiles
    in_specs = [pl.BlockSpec((PROJ_TM, D_MODEL), lambda s: (s % tiles, 0)),
                _w_rows_spec(layer, table, PROJ_TN, tiles)]
    out_specs = [pl.BlockSpec((PROJ_TM, PROJ_TN), lambda s: (s % tiles, s // tiles))]
    out_shape = [jax.ShapeDtypeStruct((m, len(table) * PROJ_TN), jnp.bfloat16)]
    scratch = [pltpu.VMEM((PROJ_TN, D_MODEL), jnp.bfloat16)]
    args = [hn, w_in_t]
    n_units = 0
    if gla is not None:
        zg, kt, gat, walt, bcol, g_gla, batch, seq = gla
        n_units = batch * GLA_HEADS
        assert n_units <= steps

        def unit(s):
            u = jnp.minimum(s, n_units - 1)
            return u // GLA_HEADS, u % GLA_HEADS

        g_in, g_out = _gla_specs(layer, seq, unit)
        in_specs += g_in
        out_specs.append(g_out)
        out_shape.append(jax.ShapeDtypeStruct((m, D_GLA), jnp.bfloat16))
        scratch += _gla_scratch(seq)
        args += [zg, kt, zg, zg, gat, walt, bcol, g_gla]
    return pl.pallas_call(
        functools.partial(_proj_nn_kernel, tiles, n_units),
        grid=(steps,),
        in_specs=in_specs,
        out_specs=out_specs,
        out_shape=out_shape,
        scratch_shapes=scratch,
        compiler_params=pltpu.CompilerParams(
            dimension_semantics=("arbitrary",),
            vmem_limit_bytes=VMEM_LIMIT_FUSED if n_units else VMEM_LIMIT),
        name="proj_att_gla" if n_units else "proj_gla",
    )(*args)


def _proj_nt_kernel(tiles, hn_ref, wt_ref, wga_ref, kt_ref, gat_ref, wbf_ref):
    s = pl.program_id(0)

    @pl.when(s % tiles == 0)
    def _():
        _cast_rows(wt_ref, wbf_ref, PROJ_TN)

    @pl.when(s == 0)
    def _():
        wbf_ref[PROJ_TN:, :] = wga_ref[...].astype(wbf_ref.dtype)

    res = lax.dot_general(wbf_ref[...], hn_ref[...], (((1,), (1,)), ((), ())),
                          preferred_element_type=jnp.float32)
    scale = jnp.where(s < tiles, GLA_DK ** -0.5, ATT_HD ** -0.5 * LOG2E)
    kt_ref[...] = (res[:PROJ_TN] * scale).astype(kt_ref.dtype)
    gat_ref[...] = res[PROJ_TN:].astype(gat_ref.dtype)


def _proj_nt(layer, hn, w_in_t):
    m = hn.shape[0]
    tiles = m // PROJ_TM
    return pl.pallas_call(
        functools.partial(_proj_nt_kernel, tiles),
        grid=(len(_NT_ROWS) * tiles,),
        in_specs=[pl.BlockSpec((PROJ_TM, D_MODEL), lambda s: (s % tiles, 0)),
                  _w_rows_spec(layer, _NT_ROWS, PROJ_TN, tiles),
                  _w_rows_spec(layer, [W_GA], GATE_RANK, len(_NT_ROWS) * tiles)],
        out_specs=[pl.BlockSpec((PROJ_TN, PROJ_TM), lambda s: (s // tiles, s % tiles)),
                   pl.BlockSpec((None, GATE_RANK, PROJ_TM), lambda s: (s // tiles, 0, s % tiles))],
        out_shape=[jax.ShapeDtypeStruct((D_KT, m), jnp.bfloat16),
                   jax.ShapeDtypeStruct((len(_NT_ROWS), GATE_RANK, m), jnp.bfloat16)],
        scratch_shapes=[pltpu.VMEM((PROJ_TN + GATE_RANK, D_MODEL), jnp.bfloat16)],
        compiler_params=pltpu.CompilerParams(
            dimension_semantics=("arbitrary",), vmem_limit_bytes=VMEM_LIMIT),
        name="proj_nt",
    )(hn, w_in_t, w_in_t)


def _out_proj_kernel(fused, n_tiles, og_ref, oa_ref, w_ref, x_ref, gpost_ref, *rest):
    if fused:
        gpre_ref, h_ref, hn_ref, wbf_ref, *y_refs = rest
    else:
        h_ref, wbf_ref, *y_refs = rest
    i = pl.program_id(0)

    @pl.when(i == 0)
    def _():
        _cast_rows(w_ref, wbf_ref, D_GLA + D_ATT)

    def matmul(y_ref):
        y_ref[...] = (jnp.dot(og_ref[...], wbf_ref[:D_GLA, :], preferred_element_type=jnp.float32)
                      + jnp.dot(oa_ref[...], wbf_ref[D_GLA:, :], preferred_element_type=jnp.float32))

    def finish(y_ref):
        y = y_ref[...]
        h = x_ref[...] + y * _rms_scale(y) * gpost_ref[...]
        if fused:
            hn_ref[...] = (h * _rms_scale(h) * gpre_ref[...]).astype(hn_ref.dtype)
        h_ref[...] = h

    pl.when(i == 0)(lambda: matmul(y_refs[0]))
    for parity in range(2):
        @pl.when((i > 0) & (i < n_tiles) & (i % 2 == parity))
        def _():
            matmul(y_refs[parity])
            finish(y_refs[1 - parity])
    pl.when(i == n_tiles)(lambda: finish(y_refs[(n_tiles - 1) % 2]))


def _out_proj(layer, og, oa, w_out, x, g_post, g_pre=None):
    m = x.shape[0]
    fused = g_pre is not None
    n_tiles = m // OUT_TM
    mm_tile = lambda i: (jnp.minimum(i, n_tiles - 1), 0)
    fin_tile = lambda i: (jnp.maximum(i - 1, 0), 0)
    in_specs = [
        pl.BlockSpec((OUT_TM, D_GLA), mm_tile),
        pl.BlockSpec((OUT_TM, D_ATT), mm_tile),
        pl.BlockSpec((None, D_GLA + D_ATT, D_MODEL), lambda i: (layer, 0, 0), pipeline_mode=pl.Buffered(1)),
        pl.BlockSpec((OUT_TM, D_MODEL), fin_tile),
        pl.BlockSpec((None, 1, D_MODEL), lambda i: (layer, 0, 0)),
    ]
    out_specs = [pl.BlockSpec((OUT_TM, D_MODEL), fin_tile)]
    out_shape = [jax.ShapeDtypeStruct((m, D_MODEL), jnp.float32)]
    args = [og, oa, w_out, x, g_post]
    if fused:
        in_specs.append(pl.BlockSpec((None, 1, D_MODEL), lambda i: (layer + 1, 0, 0)))
        out_specs.append(pl.BlockSpec((OUT_TM, D_MODEL), fin_tile))
        out_shape.append(jax.ShapeDtypeStruct((m, D_MODEL), jnp.bfloat16))
        args.append(g_pre)
    return pl.pallas_call(
        functools.partial(_out_proj_kernel, fused, n_tiles),
        grid=(n_tiles + 1,),
        in_specs=in_specs,
        out_specs=out_specs,
        out_shape=out_shape,
        scratch_shapes=[pltpu.VMEM((D_GLA + D_ATT, D_MODEL), jnp.bfloat16),
                        pltpu.VMEM((OUT_TM, D_MODEL), jnp.float32),
                        pltpu.VMEM((OUT_TM, D_MODEL), jnp.float32)],
        compiler_params=pltpu.CompilerParams(
            dimension_semantics=("arbitrary",), vmem_limit_bytes=VMEM_LIMIT),
        name="out_proj",
    )(*args)


def kernel(x, w_in, w_out, g_pre, g_post, w_alpha, b_alpha, g_gla, g_att, rel_bias):
    batch, seq, d_model = x.shape
    depth = w_in.shape[0]
    h = x.reshape(batch * seq, d_model)
    w_in_t = jnp.swapaxes(w_in, 1, 2).reshape(depth * D_IN, d_model)
    walt = jnp.pad(jnp.swapaxes(w_alpha, 1, 2), ((0, 0), (0, 0), (0, LANES - GATE_RANK))).astype(jnp.bfloat16)
    bcol = b_alpha[:, :, None]
    bias_rows, band_mask = _band_bias_rows(rel_bias), _band_mask()
    row = lambda p: p[:, None, :]
    hn = _norm_call(0, h, row(g_pre))
    for l in range(depth):
        kt, gat = _proj_nt(l, hn, w_in_t)
        (zg,) = _proj_nn(l, hn, w_in_t, _GLA_ROWS)
        za, og = _proj_nn(l, hn, w_in_t, _ATT_ROWS, gla=(zg, kt, gat, walt, bcol, row(g_gla), batch, seq))
        oa = _band_attn(l, za, kt, bias_rows, band_mask, row(g_att), batch, seq)
        if l + 1 < depth:
            h, hn = _out_proj(l, og, oa, w_out, h, row(g_post), row(g_pre))
        else:
            (h,) = _out_proj(l, og, oa, w_out, h, row(g_post))
    return h.reshape(batch, seq, d_model)
```

```python
import functools
import math

import jax
import jax.numpy as jnp
import numpy as np
from jax import lax
from jax.experimental import pallas as pl
from jax.experimental.pallas import tpu as pltpu

D_MODEL = 2048
CHUNK = 64
D_GLA = 1024
D_ATT = 1024
GLA_HEADS = 4
GLA_DK = 128
GLA_DV = 256
GLA_KW = GLA_HEADS * GLA_DK
GATE_RANK = 16
GLA_TAU = 16.0
ATT_HEADS = 8
ATT_HD = 128
LEFT_CHUNKS = 8
REL_CLIP = 128
EPS = 1e-6
LOG2E = math.log2(math.e)

LANES = 128

W_GA = 2 * GLA_KW + 2 * D_GLA
W_AQ = W_GA + GATE_RANK
D_IN = W_AQ + 4 * D_ATT
ROW_ALIGN = GATE_RANK
PROJ_TN = 512
Z_GQ, Z_GV, Z_GG = 0, GLA_KW, GLA_KW + D_GLA
Z_AQ, Z_AV, Z_AG = 0, D_ATT, 2 * D_ATT
KT_G, KT_A = 0, GLA_KW
D_KT = GLA_KW + D_ATT

PROJ_TM = 2048
MIX_PIECE_ROWS, MIX_PIECE_COLS = 256, 256
NORM_TM = 512
OUT_TM = 256
OUT_W_ROWS = 256
OUT_PIECE_COLS = 512
GLA_GROUP = 256
GLA_UNROLL = 2
ATT_TQ = 128
ATT_WIN = ATT_TQ + LEFT_CHUNKS * CHUNK
CAST_ROWS = 256
VMEM_LIMIT = 48 * 1024 * 1024
VMEM_LIMIT_FUSED = 56 * 1024 * 1024


def _silu(x):
    half = 0.5 * x
    return half + half * jnp.tanh(half)


def _log_sigmoid(x):
    return jnp.minimum(x, 0.0) - jnp.log(1.0 + jnp.exp(-jnp.abs(x)))


def _rms_scale(x):
    return lax.rsqrt(jnp.mean(x * x, axis=-1, keepdims=True) + EPS)


def _norm_kernel(x_ref, gpre_ref, hn_ref):
    x = x_ref[...]
    hn_ref[...] = (x * _rms_scale(x) * gpre_ref[...]).astype(hn_ref.dtype)


def _norm_call(layer, x, g_pre):
    rows = x.shape[0]
    return pl.pallas_call(
        _norm_kernel,
        grid=(rows // NORM_TM,),
        in_specs=[pl.BlockSpec((NORM_TM, D_MODEL), lambda i: (i, 0)),
                  pl.BlockSpec((None, 1, D_MODEL), lambda i: (layer, 0, 0))],
        out_specs=pl.BlockSpec((NORM_TM, D_MODEL), lambda i: (i, 0)),
        out_shape=jax.ShapeDtypeStruct((rows, D_MODEL), jnp.bfloat16),
        compiler_params=pltpu.CompilerParams(
            dimension_semantics=("parallel",), vmem_limit_bytes=VMEM_LIMIT),
        name="norm",
    )(x, g_pre)


def _split3_bf16(x):
    hi = x.astype(jnp.bfloat16)
    r1 = x - hi.astype(jnp.float32)
    mid = r1.astype(jnp.bfloat16)
    lo = (r1 - mid.astype(jnp.float32)).astype(jnp.bfloat16)
    return hi, mid, lo


def _gla_unit(side_work, q_ref, kt_ref, v_ref, gate_ref, gat_ref, walt_ref, bcol_ref, g_ref, o_ref,
              st_ref, parts_ref, sums_ref, u_ref, a_ref):
    seq = q_ref.shape[0]
    n_groups = seq // GLA_GROUP
    chunks_per_group = GLA_GROUP // CHUNK
    r = lax.broadcasted_iota(jnp.int32, (3 * GLA_GROUP, 2 * GLA_GROUP), 0) % GLA_GROUP
    c = lax.broadcasted_iota(jnp.int32, (3 * GLA_GROUP, 2 * GLA_GROUP), 1)
    sums = jnp.where((r // CHUNK == (c % GLA_GROUP) // CHUNK) & (r <= c), 1.0, 0.0).astype(jnp.bfloat16)
    frame_chunk = lax.broadcasted_iota(jnp.int32, (GLA_DK, GLA_GROUP), 1) // CHUNK
    ga_pad = jnp.zeros((LANES - GATE_RANK, GLA_GROUP), jnp.bfloat16)
    g = g_ref[...]

    def aligned(start, size):
        return pl.ds(start if isinstance(start, int) else pl.multiple_of(start, size), size)

    def gate_step(gi, carry):
        ga = jnp.concatenate([gat_ref[:, aligned(gi * GLA_GROUP, GLA_GROUP)], ga_pad], axis=0)
        pre = jnp.dot(walt_ref[...], ga, preferred_element_type=jnp.float32) + bcol_ref[...]
        log_a = _log_sigmoid(pre) * (1.0 / GLA_TAU)
        parts_ref[aligned(gi * GLA_DK, GLA_DK), :] = jnp.concatenate(_split3_bf16(log_a), axis=1)
        return carry

    def decay_step(gi, carry):
        cols = aligned(gi * GLA_GROUP, GLA_GROUP)
        both = sums_ref[aligned(gi * GLA_DK, GLA_DK), :]
        run, total = both[:, :GLA_GROUP], both[:, GLA_GROUP:]
        k_dec = (kt_ref[:, cols].astype(jnp.float32) * jnp.exp(total - run)).astype(jnp.bfloat16)
        k_own = jnp.concatenate([jnp.where(frame_chunk == ci, k_dec, jnp.zeros_like(k_dec))
                                 for ci in range(chunks_per_group)], axis=0)
        u = jnp.dot(k_own, v_ref[cols, :], preferred_element_type=jnp.float32)
        a = jnp.exp(total)
        for ci in range(chunks_per_group):
            u_ref[gi, ci] = u[ci * GLA_DK:(ci + 1) * GLA_DK]
            a_ref[gi, ci] = jnp.broadcast_to(a[:, ci * CHUNK:ci * CHUNK + 1], (GLA_DK, GLA_DV))
        return carry

    def state_step(gi, carry):
        s_c = st_ref[...]
        for ci in range(chunks_per_group):
            s_c = a_ref[gi, ci] * s_c + u_ref[gi, ci]
            rows = aligned(gi * GLA_GROUP + ci * CHUNK, CHUNK)
            o = jnp.dot(q_ref[rows, :], s_c.astype(jnp.bfloat16), preferred_element_type=jnp.float32)
            o = o * _rms_scale(o) * g
            o_ref[rows, :] = (o * _silu(gate_ref[rows, :].astype(jnp.float32))).astype(o_ref.dtype)
        st_ref[...] = s_c
        return carry

    def sums_step():
        sums_ref[...] = jnp.dot(parts_ref[...], sums, preferred_element_type=jnp.float32)

    stages = ([functools.partial(gate_step, gi, 0) for gi in range(n_groups)] + [sums_step]
              + [functools.partial(decay_step, gi, 0) for gi in range(n_groups)]
              + [functools.partial(state_step, gi, 0) for gi in range(n_groups)])
    side_at = {(k * len(stages)) // len(side_work): fn for k, fn in enumerate(side_work)}
    st_ref[...] = jnp.zeros_like(st_ref)
    for k, stage in enumerate(stages):
        if k in side_at:
            side_at[k]()
        stage()


def _gla_specs(layer, seq, unit):
    b_h = lambda off: (lambda *idx: (unit(*idx)[0], off + unit(*idx)[1]))
    in_specs = [
        pl.BlockSpec((seq, GLA_DK), b_h(Z_GQ // GLA_DK)),
        pl.BlockSpec((GLA_DK, seq), lambda *idx: (KT_G // GLA_DK + unit(*idx)[1], unit(*idx)[0])),
        pl.BlockSpec((seq, GLA_DV), b_h(Z_GV // GLA_DV)),
        pl.BlockSpec((seq, GLA_DV), b_h(Z_GG // GLA_DV)),
        pl.BlockSpec((None, GATE_RANK, seq), lambda *idx: (0, 0, unit(*idx)[0])),
        pl.BlockSpec((None, GLA_DK, LANES), lambda *idx: (layer, unit(*idx)[1], 0)),
        pl.BlockSpec((None, GLA_DK, 1), lambda *idx: (layer, unit(*idx)[1], 0)),
        pl.BlockSpec((None, 1, GLA_DV), lambda *idx: (layer, 0, unit(*idx)[1])),
    ]
    out_spec = pl.BlockSpec((seq, GLA_DV), lambda *idx: unit(*idx))
    return in_specs, out_spec


def _gla_scratch(seq):
    n_groups = seq // GLA_GROUP
    chunks_per_group = GLA_GROUP // CHUNK
    return [pltpu.VMEM((GLA_DK, GLA_DV), jnp.float32),
            pltpu.VMEM((n_groups * GLA_DK, 3 * GLA_GROUP), jnp.bfloat16),
            pltpu.VMEM((n_groups * GLA_DK, 2 * GLA_GROUP), jnp.float32),
            pltpu.VMEM((n_groups, chunks_per_group, GLA_DK, GLA_DV), jnp.float32),
            pltpu.VMEM((n_groups, chunks_per_group, GLA_DK, GLA_DV), jnp.float32)]


def _attn_scores(q_ref, kt_ref, bias_ref, q_start, k_start, width):
    s = jnp.dot(q_ref[q_start:q_start + ATT_TQ, :], kt_ref[:, k_start:k_start + width],
                preferred_element_type=jnp.float32)
    return s + bias_ref[:, ATT_WIN - width:]


def _attn_softmax(s2):
    m = jnp.max(s2, axis=-1, keepdims=True)
    e = jnp.exp2(s2 - m)
    return e.astype(jnp.bfloat16), jnp.sum(e, axis=-1, keepdims=True)


def _attn_finish(p, denom, v_ref, gate_ref, g, o_ref, q_start, k_start, width):
    o = jnp.dot(p, v_ref[k_start:k_start + width, :], preferred_element_type=jnp.float32)
    o = o * (1.0 / denom)
    o = o * _rms_scale(o) * g
    rows = slice(q_start, q_start + ATT_TQ)
    o_ref[rows, :] = (o * _silu(gate_ref[rows, :].astype(jnp.float32))).astype(o_ref.dtype)


def _attn_bias_tile(brow_ref, mask_ref):
    rows = jnp.broadcast_to(brow_ref[...], (ATT_TQ, ATT_TQ + ATT_WIN))
    return pltpu.roll(rows, 0, 1, stride=1, stride_axis=0)[:, :ATT_WIN] + mask_ref[...]


def _attn_unit(q_ref, kt_ref, v_ref, gate_ref, bias_ref, g_ref, o_ref, side_work=()):
    seq = q_ref.shape[0]
    g = g_ref[...]
    n_tiles = seq // ATT_TQ
    side_at = {(i * (n_tiles + 2)) // len(side_work): fn for i, fn in enumerate(side_work)} if side_work else {}

    def window(t):
        k_start = max(0, (t + 1) * ATT_TQ - ATT_WIN)
        return t * ATT_TQ, k_start, (t + 1) * ATT_TQ - k_start

    scores, probs = {}, {}
    for step in range(n_tiles + 2):
        if step in side_at:
            side_at[step]()
        if step < n_tiles:
            scores[step] = _attn_scores(q_ref, kt_ref, bias_ref, *window(step))
        if 0 <= step - 1 < n_tiles:
            probs[step - 1] = _attn_softmax(scores.pop(step - 1))
        if 0 <= step - 2:
            _attn_finish(*probs.pop(step - 2), v_ref, gate_ref, g, o_ref, *window(step - 2))


def _band_bias_rows(rel_bias):
    lead = rel_bias.shape[:-1]
    rb = rel_bias.astype(jnp.float32) * LOG2E
    far = rb[..., 2 * REL_CLIP:]
    row = jnp.concatenate([jnp.broadcast_to(far, lead + (ATT_WIN - REL_CLIP - (ATT_TQ - 1),)),
                           rb[..., 2 * REL_CLIP - 1:0:-1],
                           jnp.zeros(lead + (1,), jnp.float32),
                           jnp.broadcast_to(far, lead + (ATT_TQ - 1,))], axis=-1)
    return row[..., None, :]


def _band_mask():
    i = np.arange(ATT_TQ)[:, None]
    j = np.arange(ATT_WIN)[None, :]
    first = (i // CHUNK) * CHUNK
    in_band = (j >= first) & (j < first + (LEFT_CHUNKS + 1) * CHUNK)
    return jnp.asarray(np.where(in_band, 0.0, -1e30), jnp.float32)


_GLA_ROWS = [0] + [2 * GLA_KW + t * PROJ_TN for t in range(2 * D_GLA // PROJ_TN)]
_ATT_ROWS = ([W_AQ + t * PROJ_TN for t in range(D_ATT // PROJ_TN)]
             + [W_AQ + 2 * D_ATT + t * PROJ_TN for t in range(2 * D_ATT // PROJ_TN)])
_NT_ROWS = [GLA_KW] + [W_AQ + D_ATT + t * PROJ_TN for t in range(D_ATT // PROJ_TN)]


def _table_lookup(table, j):
    val = table[0] + j * 0
    for t in range(1, len(table)):
        val = jnp.where(j >= t, table[t], val)
    return val


def _cast_rows(src_ref, dst_ref, n_rows):
    def body(r, carry):
        rows = pl.ds(pl.multiple_of(r * CAST_ROWS, CAST_ROWS), CAST_ROWS)
        dst_ref[rows, :] = src_ref[rows, :].astype(dst_ref.dtype)
        return carry
    lax.fori_loop(0, n_rows // CAST_ROWS, body, 0)


def _w_rows_spec(layer, table, n_rows, tiles):
    units = [(layer * D_IN + r) // ROW_ALIGN for r in table]
    return pl.BlockSpec((pl.Element(n_rows), pl.Element(D_MODEL)),
                        lambda s: (_table_lookup(units, s // tiles) * ROW_ALIGN, 0))


def _proj_nn_kernel(tiles, n_units, hn_ref, wt_ref, *refs):
    if n_units:
        gla_in, (z_ref, og_ref, wbf_ref), gla_scratch = refs[:8], refs[8:11], refs[11:]
    else:
        z_ref, wbf_ref = refs
    s = pl.program_id(0)

    @pl.when(s % tiles == 0)
    def _():
        _cast_rows(wt_ref, wbf_ref, PROJ_TN)

    def matmul_piece(r0, n_rows, c0, n_cols):
        def run():
            rows, cols = slice(r0, r0 + n_rows), slice(c0, c0 + n_cols)
            z_ref[rows, cols] = lax.dot_general(hn_ref[rows, :], wbf_ref[cols, :], (((1,), (1,)), ((), ())),
                                                preferred_element_type=jnp.float32).astype(z_ref.dtype)
        return run

    matmul = matmul_piece(0, PROJ_TM, 0, PROJ_TN)

    if n_units:
        @pl.when(s < n_units)
        def _():
            pieces = [matmul_piece(r0, MIX_PIECE_ROWS, c0, MIX_PIECE_COLS)
                      for r0 in range(0, PROJ_TM, MIX_PIECE_ROWS) for c0 in range(0, PROJ_TN, MIX_PIECE_COLS)]
            _gla_unit(pieces, *gla_in, og_ref, *gla_scratch)

        pl.when(s >= n_units)(matmul)
    else:
        matmul()


def _proj_nn(layer, hn, w_in_t, table, gla=None):
    m = hn.shape[0]
    tiles = m // PROJ_TM
    steps = len(table) * tiles
    in_specs = [pl.BlockSpec((PROJ_TM, D_MODEL), lambda s: (s % tiles, 0)),
                _w_rows_spec(layer, table, PROJ_TN, tiles)]
    out_specs = [pl.BlockSpec((PROJ_TM, PROJ_TN), lambda s: (s % tiles, s // tiles))]
    out_shape = [jax.ShapeDtypeStruct((m, len(table) * PROJ_TN), jnp.bfloat16)]
    scratch = [pltpu.VMEM((PROJ_TN, D_MODEL), jnp.bfloat16)]
    args = [hn, w_in_t]
    n_units = 0
    if gla is not None:
        zg, kt, gat, walt, bcol, g_gla, batch, seq = gla
        n_units = batch * GLA_HEADS
        assert n_units <= steps

        def unit(s):
            u = jnp.minimum(s, n_units - 1)
            return u // GLA_HEADS, u % GLA_HEADS

        g_in, g_out = _gla_specs(layer, seq, unit)
        in_specs += g_in
        out_specs.append(g_out)
        out_shape.append(jax.ShapeDtypeStruct((m, D_GLA), jnp.bfloat16))
        scratch += _gla_scratch(seq)
        args += [zg, kt, zg, zg, gat, walt, bcol, g_gla]
    return pl.pallas_call(
        functools.partial(_proj_nn_kernel, tiles, n_units),
        grid=(steps,),
        in_specs=in_specs,
        out_specs=out_specs,
        out_shape=out_shape,
        scratch_shapes=scratch,
        compiler_params=pltpu.CompilerParams(
            dimension_semantics=("arbitrary",),
            vmem_limit_bytes=VMEM_LIMIT_FUSED if n_units else VMEM_LIMIT),
        name="proj_att_gla" if n_units else "proj_gla",
    )(*args)


def _proj_nt_kernel(tiles, hn_ref, wt_ref, wga_ref, kt_ref, gat_ref, wbf_ref):
    s = pl.program_id(0)

    @pl.when(s % tiles == 0)
    def _():
        _cast_rows(wt_ref, wbf_ref, PROJ_TN)

    @pl.when(s == 0)
    def _():
        wbf_ref[PROJ_TN:, :] = wga_ref[...].astype(wbf_ref.dtype)

    res = lax.dot_general(wbf_ref[...], hn_ref[...], (((1,), (1,)), ((), ())),
                          preferred_element_type=jnp.float32)
    scale = jnp.where(s < tiles, GLA_DK ** -0.5, ATT_HD ** -0.5 * LOG2E)
    kt_ref[...] = (res[:PROJ_TN] * scale).astype(kt_ref.dtype)
    gat_ref[...] = res[PROJ_TN:].astype(gat_ref.dtype)


def _proj_nt(layer, hn, w_in_t):
    m = hn.shape[0]
    tiles = m // PROJ_TM
    return pl.pallas_call(
        functools.partial(_proj_nt_kernel, tiles),
        grid=(len(_NT_ROWS) * tiles,),
        in_specs=[pl.BlockSpec((PROJ_TM, D_MODEL), lambda s: (s % tiles, 0)),
                  _w_rows_spec(layer, _NT_ROWS, PROJ_TN, tiles),
                  _w_rows_spec(layer, [W_GA], GATE_RANK, len(_NT_ROWS) * tiles)],
        out_specs=[pl.BlockSpec((PROJ_TN, PROJ_TM), lambda s: (s // tiles, s % tiles)),
                   pl.BlockSpec((None, GATE_RANK, PROJ_TM), lambda s: (s // tiles, 0, s % tiles))],
        out_shape=[jax.ShapeDtypeStruct((D_KT, m), jnp.bfloat16),
                   jax.ShapeDtypeStruct((len(_NT_ROWS), GATE_RANK, m), jnp.bfloat16)],
        scratch_shapes=[pltpu.VMEM((PROJ_TN + GATE_RANK, D_MODEL), jnp.bfloat16)],
        compiler_params=pltpu.CompilerParams(
            dimension_semantics=("arbitrary",), vmem_limit_bytes=VMEM_LIMIT),
        name="proj_nt",
    )(hn, w_in_t, w_in_t)


def _attn_out_kernel(fused_norm, n_cast, batch, q_ref, kt_ref, v_ref, gate_ref, brow_ref, mask_ref, gatt_ref,
                     og_ref, w_ref, x_ref, gpost_ref, *rest):
    if fused_norm:
        gpre_ref, h_ref, hn_ref, wbf_ref, bias_ref, *bufs = rest
    else:
        h_ref, wbf_ref, bias_ref, *bufs = rest
    y_refs, oa_refs = bufs[:2], bufs[2:]
    seq = q_ref.shape[0]
    tiles_per_b = seq // OUT_TM
    assert tiles_per_b == ATT_HEADS
    units, n_tiles = batch * ATT_HEADS, batch * tiles_per_b
    s = pl.program_id(0)
    k = s - n_cast
    head = k % ATT_HEADS

    @pl.when(s == 0)
    def _():
        y_refs[1][...] = jnp.zeros_like(y_refs[1])

    @pl.when(s < n_cast)
    def _():
        rows = pl.ds(pl.multiple_of(s * OUT_W_ROWS, OUT_W_ROWS), OUT_W_ROWS)
        wbf_ref[rows, :] = w_ref[...].astype(wbf_ref.dtype)

    @pl.when((k >= 0) & (k < ATT_HEADS))
    def _():
        bias_ref[head] = _attn_bias_tile(brow_ref, mask_ref)

    def attend(oa_ref, side_work=()):
        _attn_unit(q_ref, kt_ref, v_ref, gate_ref, bias_ref.at[head], gatt_ref, oa_ref.at[head], side_work)

    def matmul_pieces(oa_ref, y_ref):
        def piece(c0):
            def run():
                rows = pl.ds(pl.multiple_of(head * OUT_TM, OUT_TM), OUT_TM)
                oa = jnp.concatenate([oa_ref[hd, rows, :] for hd in range(ATT_HEADS)], axis=1)
                cols = slice(c0, c0 + OUT_PIECE_COLS)
                y_ref[:, cols] = (jnp.dot(og_ref[...], wbf_ref[:D_GLA, cols], preferred_element_type=jnp.float32)
                                  + jnp.dot(oa, wbf_ref[D_GLA:, cols], preferred_element_type=jnp.float32))
            return run
        return [piece(c0) for c0 in range(0, D_MODEL, OUT_PIECE_COLS)]

    def matmul(oa_ref, y_ref):
        for run in matmul_pieces(oa_ref, y_ref):
            run()

    def finish(y_ref):
        y = y_ref[...]
        h = x_ref[...] + y * _rms_scale(y) * gpost_ref[...]
        if fused_norm:
            hn_ref[...] = (h * _rms_scale(h) * gpre_ref[...]).astype(hn_ref.dtype)
        h_ref[...] = h

    pl.when((k >= 0) & (k < ATT_HEADS))(lambda: attend(oa_refs[0]))
    for b_par in range(2):
        for y_par in range(2):
            @pl.when((k >= ATT_HEADS) & (k < units) & ((k // ATT_HEADS) % 2 == b_par) & (k % 2 == y_par))
            def _():
                attend(oa_refs[b_par], matmul_pieces(oa_refs[1 - b_par], y_refs[y_par]))
                finish(y_refs[1 - y_par])
    last_oa = oa_refs[(batch - 1) % 2]
    for y_par in range(2):
        @pl.when((k >= units) & (k < ATT_HEADS + n_tiles) & (k % 2 == y_par))
        def _():
            matmul(last_oa, y_refs[y_par])
            finish(y_refs[1 - y_par])
    pl.when(k == ATT_HEADS + n_tiles)(lambda: finish(y_refs[(ATT_HEADS + n_tiles - 1) % 2]))


def _attn_out(layer, za, kt, bias_rows, band_mask, g_att, og, w_out, x, g_post, batch, seq, g_pre=None):
    m = x.shape[0]
    fused_norm = g_pre is not None
    n_tiles = m // OUT_TM
    units = batch * ATT_HEADS
    n_cast = (D_GLA + D_ATT) // OUT_W_ROWS
    assert n_cast % 2 == 0 and ATT_HEADS % 2 == 0
    blk = lambda off: off // ATT_HD

    def unit(s):
        u = jnp.clip(s - n_cast, 0, units - 1)
        return u // ATT_HEADS, u % ATT_HEADS

    b_h = lambda off: (lambda s: (unit(s)[0], off + unit(s)[1]))
    mm_tile = lambda s: (jnp.clip(s - n_cast - ATT_HEADS, 0, n_tiles - 1), 0)
    fin_tile = lambda s: (jnp.clip(s - n_cast - ATT_HEADS - 1, 0, n_tiles - 1), 0)
    in_specs = [
        pl.BlockSpec((seq, ATT_HD), b_h(blk(Z_AQ))),
        pl.BlockSpec((ATT_HD, seq), lambda s: (blk(KT_A) + unit(s)[1], unit(s)[0])),
        pl.BlockSpec((seq, ATT_HD), b_h(blk(Z_AV))),
        pl.BlockSpec((seq, ATT_HD), b_h(blk(Z_AG))),
        pl.BlockSpec((None, None, 1, ATT_TQ + ATT_WIN), lambda s: (layer, unit(s)[1], 0, 0)),
        pl.BlockSpec((ATT_TQ, ATT_WIN), lambda s: (0, 0)),
        pl.BlockSpec((None, 1, ATT_HD), lambda s: (layer, 0, unit(s)[1])),
        pl.BlockSpec((OUT_TM, D_GLA), mm_tile),
        pl.BlockSpec((None, OUT_W_ROWS, D_MODEL), lambda s: (layer, jnp.minimum(s, n_cast - 1), 0)),
        pl.BlockSpec((OUT_TM, D_MODEL), fin_tile),
        pl.BlockSpec((None, 1, D_MODEL), lambda s: (layer, 0, 0)),
    ]
    out_specs = [pl.BlockSpec((OUT_TM, D_MODEL), fin_tile)]
    out_shape = [jax.ShapeDtypeStruct((m, D_MODEL), jnp.float32)]
    args = [za, kt, za, za, bias_rows, band_mask, g_att, og, w_out, x, g_post]
    if fused_norm:
        in_specs.append(pl.BlockSpec((None, 1, D_MODEL), lambda s: (layer + 1, 0, 0)))
        out_specs.append(pl.BlockSpec((OUT_TM, D_MODEL), fin_tile))
        out_shape.append(jax.ShapeDtypeStruct((m, D_MODEL), jnp.bfloat16))
        args.append(g_pre)
    return pl.pallas_call(
        functools.partial(_attn_out_kernel, fused_norm, n_cast, batch),
        grid=(n_cast + ATT_HEADS + n_tiles + 1,),
        in_specs=in_specs,
        out_specs=out_specs,
        out_shape=out_shape,
        scratch_shapes=[pltpu.VMEM((D_GLA + D_ATT, D_MODEL), jnp.bfloat16),
                        pltpu.VMEM((ATT_HEADS, ATT_TQ, ATT_WIN), jnp.float32),
                        pltpu.VMEM((OUT_TM, D_MODEL), jnp.float32),
                        pltpu.VMEM((OUT_TM, D_MODEL), jnp.float32),
                        pltpu.VMEM((ATT_HEADS, seq, ATT_HD), jnp.bfloat16),
                        pltpu.VMEM((ATT_HEADS, seq, ATT_HD), jnp.bfloat16)],
        compiler_params=pltpu.CompilerParams(
            dimension_semantics=("arbitrary",), vmem_limit_bytes=VMEM_LIMIT),
        name="attn_out",
    )(*args)


def kernel(x, w_in, w_out, g_pre, g_post, w_alpha, b_alpha, g_gla, g_att, rel_bias):
    batch, seq, d_model = x.shape
    depth = w_in.shape[0]
    h = x.reshape(batch * seq, d_model)
    w_in_t = jnp.swapaxes(w_in, 1, 2).reshape(depth * D_IN, d_model)
    walt = jnp.pad(jnp.swapaxes(w_alpha, 1, 2), ((0, 0), (0, 0), (0, LANES - GATE_RANK))).astype(jnp.bfloat16)
    bcol = b_alpha[:, :, None]
    bias_rows, band_mask = _band_bias_rows(rel_bias), _band_mask()
    row = lambda p: p[:, None, :]
    hn = _norm_call(0, h, row(g_pre))
    for l in range(depth):
        kt, gat = _proj_nt(l, hn, w_in_t)
        (zg,) = _proj_nn(l, hn, w_in_t, _GLA_ROWS)
        za, og = _proj_nn(l, hn, w_in_t, _ATT_ROWS, gla=(zg, kt, gat, walt, bcol, row(g_gla), batch, seq))
        mixed = (l, za, kt, bias_rows, band_mask, row(g_att), og, w_out, h, row(g_post), batch, seq)
        if l + 1 < depth:
            h, hn = _attn_out(*mixed, g_pre=row(g_pre))
        else:
            (h,) = _attn_out(*mixed)
    return h.reshape(batch, seq, d_model)
```

```python
import functools
import math

import jax
import jax.numpy as jnp
import numpy as np
from jax import lax
from jax.experimental import pallas as pl
from jax.experimental.pallas import tpu as pltpu

D_MODEL = 2048
CHUNK = 64
D_GLA = 1024
D_ATT = 1024
GLA_HEADS = 4
GLA_DK = 128
GLA_DV = 256
GLA_KW = GLA_HEADS * GLA_DK
GATE_RANK = 16
GLA_TAU = 16.0
ATT_HEADS = 8
ATT_HD = 128
LEFT_CHUNKS = 8
REL_CLIP = 128
EPS = 1e-6
LOG2E = math.log2(math.e)

LANES = 128

W_GA = 2 * GLA_KW + 2 * D_GLA
W_AQ = W_GA + GATE_RANK
D_IN = W_AQ + 4 * D_ATT
ROW_ALIGN = GATE_RANK
PROJ_TN = 512
Z_GQ, Z_GV, Z_GG = 0, GLA_KW, GLA_KW + D_GLA
Z_AQ, Z_AV, Z_AG = 0, D_ATT, 2 * D_ATT
KT_G, KT_A = 0, GLA_KW
D_KT = GLA_KW + D_ATT

PROJ_TM = 2048
MIX_PIECE_ROWS, MIX_PIECE_COLS = 256, 256
NORM_TM = 512
OUT_TM = 256
OUT_W_ROWS = 256
GLA_GROUP = 256
ATT_TQ = 128
ATT_WIN = ATT_TQ + LEFT_CHUNKS * CHUNK
CAST_ROWS = 256
VMEM_LIMIT = 48 * 1024 * 1024
VMEM_LIMIT_FUSED = 56 * 1024 * 1024


def _silu(x):
    half = 0.5 * x
    return half + half * jnp.tanh(half)


def _log_sigmoid(x):
    return jnp.minimum(x, 0.0) - jnp.log(1.0 + jnp.exp(-jnp.abs(x)))


def _rms_scale(x):
    return lax.rsqrt(jnp.mean(x * x, axis=-1, keepdims=True) + EPS)


def _norm_kernel(x_ref, gpre_ref, hn_ref):
    x = x_ref[...]
    hn_ref[...] = (x * _rms_scale(x) * gpre_ref[...]).astype(hn_ref.dtype)


def _norm_call(layer, x, g_pre):
    rows = x.shape[0]
    return pl.pallas_call(
        _norm_kernel,
        grid=(rows // NORM_TM,),
        in_specs=[pl.BlockSpec((NORM_TM, D_MODEL), lambda i: (i, 0)),
                  pl.BlockSpec((None, 1, D_MODEL), lambda i: (layer, 0, 0))],
        out_specs=pl.BlockSpec((NORM_TM, D_MODEL), lambda i: (i, 0)),
        out_shape=jax.ShapeDtypeStruct((rows, D_MODEL), jnp.bfloat16),
        compiler_params=pltpu.CompilerParams(
            dimension_semantics=("parallel",), vmem_limit_bytes=VMEM_LIMIT),
        name="norm",
    )(x, g_pre)


def _split3_bf16(x):
    hi = x.astype(jnp.bfloat16)
    r1 = x - hi.astype(jnp.float32)
    mid = r1.astype(jnp.bfloat16)
    lo = (r1 - mid.astype(jnp.float32)).astype(jnp.bfloat16)
    return hi, mid, lo


def _gla_unit(side_work, q_ref, kt_ref, v_ref, gate_ref, gat_ref, walt_ref, bcol_ref, g_ref, o_ref,
              st_ref, parts_ref, sums_ref, u_ref, a_ref):
    seq = q_ref.shape[0]
    n_groups = seq // GLA_GROUP
    chunks_per_group = GLA_GROUP // CHUNK
    r = lax.broadcasted_iota(jnp.int32, (3 * GLA_GROUP, 2 * GLA_GROUP), 0) % GLA_GROUP
    c = lax.broadcasted_iota(jnp.int32, (3 * GLA_GROUP, 2 * GLA_GROUP), 1)
    sums = jnp.where((r // CHUNK == (c % GLA_GROUP) // CHUNK) & (r <= c), 1.0, 0.0).astype(jnp.bfloat16)
    frame_chunk = lax.broadcasted_iota(jnp.int32, (GLA_DK, GLA_GROUP), 1) // CHUNK
    ga_pad = jnp.zeros((LANES - GATE_RANK, GLA_GROUP), jnp.bfloat16)
    g = g_ref[...]

    def gate_step(gi):
        ga = jnp.concatenate([gat_ref[:, pl.ds(gi * GLA_GROUP, GLA_GROUP)], ga_pad], axis=0)
        pre = jnp.dot(walt_ref[...], ga, preferred_element_type=jnp.float32) + bcol_ref[...]
        log_a = _log_sigmoid(pre) * (1.0 / GLA_TAU)
        parts_ref[pl.ds(gi * GLA_DK, GLA_DK), :] = jnp.concatenate(_split3_bf16(log_a), axis=1)

    def sums_step():
        sums_ref[...] = jnp.dot(parts_ref[...], sums, preferred_element_type=jnp.float32)

    def decay_step(gi):
        cols = pl.ds(gi * GLA_GROUP, GLA_GROUP)
        both = sums_ref[pl.ds(gi * GLA_DK, GLA_DK), :]
        run, total = both[:, :GLA_GROUP], both[:, GLA_GROUP:]
        k_dec = (kt_ref[:, cols].astype(jnp.float32) * jnp.exp(total - run)).astype(jnp.bfloat16)
        k_own = jnp.concatenate([jnp.where(frame_chunk == ci, k_dec, jnp.zeros_like(k_dec))
                                 for ci in range(chunks_per_group)], axis=0)
        u = jnp.dot(k_own, v_ref[cols, :], preferred_element_type=jnp.float32)
        a = jnp.exp(total)
        for ci in range(chunks_per_group):
            u_ref[gi, ci] = u[ci * GLA_DK:(ci + 1) * GLA_DK]
            a_ref[gi, ci] = jnp.broadcast_to(a[:, ci * CHUNK:ci * CHUNK + 1], (GLA_DK, GLA_DV))

    def state_step(gi):
        s_c = st_ref[...]
        for ci in range(chunks_per_group):
            s_c = a_ref[gi, ci] * s_c + u_ref[gi, ci]
            rows = pl.ds(gi * GLA_GROUP + ci * CHUNK, CHUNK)
            o = jnp.dot(q_ref[rows, :], s_c.astype(jnp.bfloat16), preferred_element_type=jnp.float32)
            o = o * _rms_scale(o) * g
            o_ref[rows, :] = (o * _silu(gate_ref[rows, :].astype(jnp.float32))).astype(o_ref.dtype)
        st_ref[...] = s_c

    stages = ([functools.partial(gate_step, gi) for gi in range(n_groups)] + [sums_step]
              + [functools.partial(decay_step, gi) for gi in range(n_groups)]
              + [functools.partial(state_step, gi) for gi in range(n_groups)])
    side_at = {(k * len(stages)) // len(side_work): fn for k, fn in enumerate(side_work)}
    st_ref[...] = jnp.zeros_like(st_ref)
    for k, stage in enumerate(stages):
        if k in side_at:
            side_at[k]()
        stage()


def _gla_specs(layer, seq, unit):
    b_h = lambda off: (lambda *idx: (unit(*idx)[0], off + unit(*idx)[1]))
    in_specs = [
        pl.BlockSpec((seq, GLA_DK), b_h(Z_GQ // GLA_DK)),
        pl.BlockSpec((GLA_DK, seq), lambda *idx: (KT_G // GLA_DK + unit(*idx)[1], unit(*idx)[0])),
        pl.BlockSpec((seq, GLA_DV), b_h(Z_GV // GLA_DV)),
        pl.BlockSpec((seq, GLA_DV), b_h(Z_GG // GLA_DV)),
        pl.BlockSpec((None, GATE_RANK, seq), lambda *idx: (0, 0, unit(*idx)[0])),
        pl.BlockSpec((None, GLA_DK, LANES), lambda *idx: (layer, unit(*idx)[1], 0)),
        pl.BlockSpec((None, GLA_DK, 1), lambda *idx: (layer, unit(*idx)[1], 0)),
        pl.BlockSpec((None, 1, GLA_DV), lambda *idx: (layer, 0, unit(*idx)[1])),
    ]
    out_spec = pl.BlockSpec((seq, GLA_DV), lambda *idx: unit(*idx))
    return in_specs, out_spec


def _gla_scratch(seq):
    n_groups = seq // GLA_GROUP
    chunks_per_group = GLA_GROUP // CHUNK
    return [pltpu.VMEM((GLA_DK, GLA_DV), jnp.float32),
            pltpu.VMEM((n_groups * GLA_DK, 3 * GLA_GROUP), jnp.bfloat16),
            pltpu.VMEM((n_groups * GLA_DK, 2 * GLA_GROUP), jnp.float32),
            pltpu.VMEM((n_groups, chunks_per_group, GLA_DK, GLA_DV), jnp.float32),
            pltpu.VMEM((n_groups, chunks_per_group, GLA_DK, GLA_DV), jnp.float32)]


def _attn_scores(q_ref, kt_ref, bias_ref, q_start, k_start, width):
    s = jnp.dot(q_ref[q_start:q_start + ATT_TQ, :], kt_ref[:, k_start:k_start + width],
                preferred_element_type=jnp.float32)
    return s + bias_ref[:, ATT_WIN - width:]


def _attn_softmax(s2):
    m = jnp.max(s2, axis=-1, keepdims=True)
    e = jnp.exp2(s2 - m)
    return e.astype(jnp.bfloat16), jnp.sum(e, axis=-1, keepdims=True)


def _attn_finish(p, denom, v_ref, gate_ref, g, o_ref, q_start, k_start, width):
    o = jnp.dot(p, v_ref[k_start:k_start + width, :], preferred_element_type=jnp.float32)
    o = o * (1.0 / denom)
    o = o * _rms_scale(o) * g
    rows = slice(q_start, q_start + ATT_TQ)
    o_ref[rows, :] = (o * _silu(gate_ref[rows, :].astype(jnp.float32))).astype(o_ref.dtype)


def _attn_kernel(q_ref, kt_ref, v_ref, gate_ref, brow_ref, mask_ref, g_ref, o_ref, bias_ref):
    seq = q_ref.shape[0]
    g = g_ref[...]
    n_tiles = seq // ATT_TQ

    @pl.when(pl.program_id(1) == 0)
    def _():
        rows = jnp.broadcast_to(brow_ref[...], (ATT_TQ, ATT_TQ + ATT_WIN))
        bias_ref[...] = pltpu.roll(rows, 0, 1, stride=1, stride_axis=0)[:, :ATT_WIN] + mask_ref[...]

    def window(t):
        k_start = max(0, (t + 1) * ATT_TQ - ATT_WIN)
        return t * ATT_TQ, k_start, (t + 1) * ATT_TQ - k_start

    scores, probs = {}, {}
    for step in range(n_tiles + 2):
        if step < n_tiles:
            scores[step] = _attn_scores(q_ref, kt_ref, bias_ref, *window(step))
        if 0 <= step - 1 < n_tiles:
            probs[step - 1] = _attn_softmax(scores.pop(step - 1))
        if 0 <= step - 2:
            _attn_finish(*probs.pop(step - 2), v_ref, gate_ref, g, o_ref, *window(step - 2))


def _band_attn(layer, z, kt, bias_rows, band_mask, g_att, batch, seq):
    blk = lambda off: off // ATT_HD
    b_h = lambda off: (lambda h, b: (b, off + h))
    return pl.pallas_call(
        _attn_kernel,
        grid=(ATT_HEADS, batch),
        in_specs=[
            pl.BlockSpec((seq, ATT_HD), b_h(blk(Z_AQ))),
            pl.BlockSpec((ATT_HD, seq), lambda h, b: (blk(KT_A) + h, b)),
            pl.BlockSpec((seq, ATT_HD), b_h(blk(Z_AV))),
            pl.BlockSpec((seq, ATT_HD), b_h(blk(Z_AG))),
            pl.BlockSpec((None, None, 1, ATT_TQ + ATT_WIN), lambda h, b: (layer, h, 0, 0)),
            pl.BlockSpec((ATT_TQ, ATT_WIN), lambda h, b: (0, 0)),
            pl.BlockSpec((None, 1, ATT_HD), lambda h, b: (layer, 0, h)),
        ],
        out_specs=pl.BlockSpec((seq, ATT_HD), lambda h, b: (b, h)),
        out_shape=jax.ShapeDtypeStruct((z.shape[0], D_ATT), jnp.bfloat16),
        scratch_shapes=[pltpu.VMEM((ATT_TQ, ATT_WIN), jnp.float32)],
        compiler_params=pltpu.CompilerParams(
            dimension_semantics=("arbitrary", "arbitrary"), vmem_limit_bytes=VMEM_LIMIT),
        name="band_attn",
    )(z, kt, z, z, bias_rows, band_mask, g_att)


def _band_bias_rows(rel_bias):
    lead = rel_bias.shape[:-1]
    rb = rel_bias.astype(jnp.float32) * LOG2E
    far = rb[..., 2 * REL_CLIP:]
    row = jnp.concatenate([jnp.broadcast_to(far, lead + (ATT_WIN - REL_CLIP - (ATT_TQ - 1),)),
                           rb[..., 2 * REL_CLIP - 1:0:-1],
                           jnp.zeros(lead + (1,), jnp.float32),
                           jnp.broadcast_to(far, lead + (ATT_TQ - 1,))], axis=-1)
    return row[..., None, :]


def _band_mask():
    i = np.arange(ATT_TQ)[:, None]
    j = np.arange(ATT_WIN)[None, :]
    first = (i // CHUNK) * CHUNK
    in_band = (j >= first) & (j < first + (LEFT_CHUNKS + 1) * CHUNK)
    return jnp.asarray(np.where(in_band, 0.0, -1e30), jnp.float32)


_GLA_ROWS = [0] + [2 * GLA_KW + t * PROJ_TN for t in range(2 * D_GLA // PROJ_TN)]
_ATT_ROWS = ([W_AQ + t * PROJ_TN for t in range(D_ATT // PROJ_TN)]
             + [W_AQ + 2 * D_ATT + t * PROJ_TN for t in range(2 * D_ATT // PROJ_TN)])
_NT_ROWS = [GLA_KW] + [W_AQ + D_ATT + t * PROJ_TN for t in range(D_ATT // PROJ_TN)]


def _table_lookup(table, j):
    val = table[0] + j * 0
    for t in range(1, len(table)):
        val = jnp.where(j >= t, table[t], val)
    return val


def _cast_rows(src_ref, dst_ref, n_rows):
    def body(r, carry):
        rows = pl.ds(pl.multiple_of(r * CAST_ROWS, CAST_ROWS), CAST_ROWS)
        dst_ref[rows, :] = src_ref[rows, :].astype(dst_ref.dtype)
        return carry
    lax.fori_loop(0, n_rows // CAST_ROWS, body, 0)


def _w_rows_spec(layer, table, n_rows, tiles):
    units = [(layer * D_IN + r) // ROW_ALIGN for r in table]
    return pl.BlockSpec((pl.Element(n_rows), pl.Element(D_MODEL)),
                        lambda s: (_table_lookup(units, s // tiles) * ROW_ALIGN, 0))


def _proj_nn_kernel(tiles, n_units, hn_ref, wt_ref, *refs):
    if n_units:
        gla_in, (z_ref, og_ref, wbf_ref), gla_scratch = refs[:8], refs[8:11], refs[11:]
    else:
        z_ref, wbf_ref = refs
    s = pl.program_id(0)

    @pl.when(s % tiles == 0)
    def _():
        _cast_rows(wt_ref, wbf_ref, PROJ_TN)

    def matmul_piece(r0, n_rows, c0, n_cols):
        def run():
            rows, cols = slice(r0, r0 + n_rows), slice(c0, c0 + n_cols)
            z_ref[rows, cols] = lax.dot_general(hn_ref[rows, :], wbf_ref[cols, :], (((1,), (1,)), ((), ())),
                                                preferred_element_type=jnp.float32).astype(z_ref.dtype)
        return run

    matmul = matmul_piece(0, PROJ_TM, 0, PROJ_TN)

    if n_units:
        @pl.when(s < n_units)
        def _():
            pieces = [matmul_piece(r0, MIX_PIECE_ROWS, c0, MIX_PIECE_COLS)
                      for r0 in range(0, PROJ_TM, MIX_PIECE_ROWS) for c0 in range(0, PROJ_TN, MIX_PIECE_COLS)]
            _gla_unit(pieces, *gla_in, og_ref, *gla_scratch)

        pl.when(s >= n_units)(matmul)
    else:
        matmul()


def _proj_nn(layer, hn, w_in_t, table, gla=None):
    m = hn.shape[0]
    tiles = m // PROJ_TM
    steps = len(table) * tiles
    in_specs = [pl.BlockSpec((PROJ_TM, D_MODEL), lambda s: (s % tiles, 0)),
                _w_rows_spec(layer, table, PROJ_TN, tiles)]
    out_specs = [pl.BlockSpec((PROJ_TM, PROJ_TN), lambda s: (s % tiles, s // tiles))]
    out_shape = [jax.ShapeDtypeStruct((m, len(table) * PROJ_TN), jnp.bfloat16)]
    scratch = [pltpu.VMEM((PROJ_TN, D_MODEL), jnp.bfloat16)]
    args = [hn, w_in_t]
    n_units = 0
    if gla is not None:
        zg, kt, gat, walt, bcol, g_gla, batch, seq = gla
        n_units = batch * GLA_HEADS
        assert n_units <= steps

        def unit(s):
            u = jnp.minimum(s, n_units - 1)
            return u // GLA_HEADS, u % GLA_HEADS

        g_in, g_out = _gla_specs(layer, seq, unit)
        in_specs += g_in
        out_specs.append(g_out)
        out_shape.append(jax.ShapeDtypeStruct((m, D_GLA), jnp.bfloat16))
        scratch += _gla_scratch(seq)
        args += [zg, kt, zg, zg, gat, walt, bcol, g_gla]
    return pl.pallas_call(
        functools.partial(_proj_nn_kernel, tiles, n_units),
        grid=(steps,),
        in_specs=in_specs,
        out_specs=out_specs,
        out_shape=out_shape,
        scratch_shapes=scratch,
        compiler_params=pltpu.CompilerParams(
            dimension_semantics=("arbitrary",),
            vmem_limit_bytes=VMEM_LIMIT_FUSED if n_units else VMEM_LIMIT),
        name="proj_att_gla" if n_units else "proj_gla",
    )(*args)


def _proj_nt_kernel(tiles, hn_ref, wt_ref, wga_ref, kt_ref, gat_ref, wbf_ref):
    s = pl.program_id(0)

    @pl.when(s % tiles == 0)
    def _():
        _cast_rows(wt_ref, wbf_ref, PROJ_TN)

    @pl.when(s == 0)
    def _():
        wbf_ref[PROJ_TN:, :] = wga_ref[...].astype(wbf_ref.dtype)

    res = lax.dot_general(wbf_ref[...], hn_ref[...], (((1,), (1,)), ((), ())),
                          preferred_element_type=jnp.float32)
    scale = jnp.where(s < tiles, GLA_DK ** -0.5, ATT_HD ** -0.5 * LOG2E)
    kt_ref[...] = (res[:PROJ_TN] * scale).astype(kt_ref.dtype)
    gat_ref[...] = res[PROJ_TN:].astype(gat_ref.dtype)


def _proj_nt(layer, hn, w_in_t):
    m = hn.shape[0]
    tiles = m // PROJ_TM
    return pl.pallas_call(
        functools.partial(_proj_nt_kernel, tiles),
        grid=(len(_NT_ROWS) * tiles,),
        in_specs=[pl.BlockSpec((PROJ_TM, D_MODEL), lambda s: (s % tiles, 0)),
                  _w_rows_spec(layer, _NT_ROWS, PROJ_TN, tiles),
                  _w_rows_spec(layer, [W_GA], GATE_RANK, len(_NT_ROWS) * tiles)],
        out_specs=[pl.BlockSpec((PROJ_TN, PROJ_TM), lambda s: (s // tiles, s % tiles)),
                   pl.BlockSpec((None, GATE_RANK, PROJ_TM), lambda s: (s // tiles, 0, s % tiles))],
        out_shape=[jax.ShapeDtypeStruct((D_KT, m), jnp.bfloat16),
                   jax.ShapeDtypeStruct((len(_NT_ROWS), GATE_RANK, m), jnp.bfloat16)],
        scratch_shapes=[pltpu.VMEM((PROJ_TN + GATE_RANK, D_MODEL), jnp.bfloat16)],
        compiler_params=pltpu.CompilerParams(
            dimension_semantics=("arbitrary",), vmem_limit_bytes=VMEM_LIMIT),
        name="proj_nt",
    )(hn, w_in_t, w_in_t)


def _out_proj_kernel(fused, n_cast, n_tiles, og_ref, oa_ref, w_ref, x_ref, gpost_ref, *rest):
    if fused:
        gpre_ref, h_ref, hn_ref, wbf_ref, *y_refs = rest
    else:
        h_ref, wbf_ref, *y_refs = rest
    s = pl.program_id(0)
    i = s - n_cast

    @pl.when(s < n_cast)
    def _():
        rows = pl.ds(pl.multiple_of(s * OUT_W_ROWS, OUT_W_ROWS), OUT_W_ROWS)
        wbf_ref[rows, :] = w_ref[...].astype(wbf_ref.dtype)

    def matmul(y_ref):
        y_ref[...] = (jnp.dot(og_ref[...], wbf_ref[:D_GLA, :], preferred_element_type=jnp.float32)
                      + jnp.dot(oa_ref[...], wbf_ref[D_GLA:, :], preferred_element_type=jnp.float32))

    def finish(y_ref):
        y = y_ref[...]
        h = x_ref[...] + y * _rms_scale(y) * gpost_ref[...]
        if fused:
            hn_ref[...] = (h * _rms_scale(h) * gpre_ref[...]).astype(hn_ref.dtype)
        h_ref[...] = h

    pl.when(i == 0)(lambda: matmul(y_refs[0]))
    for parity in range(2):
        @pl.when((i > 0) & (i < n_tiles) & (i % 2 == parity))
        def _():
            matmul(y_refs[parity])
            finish(y_refs[1 - parity])
    pl.when(i == n_tiles)(lambda: finish(y_refs[(n_tiles - 1) % 2]))


def _out_proj(layer, og, oa, w_out, x, g_post, g_pre=None):
    m = x.shape[0]
    fused = g_pre is not None
    n_tiles = m // OUT_TM
    n_cast = (D_GLA + D_ATT) // OUT_W_ROWS
    assert n_cast % 2 == 0
    mm_tile = lambda s: (jnp.clip(s - n_cast, 0, n_tiles - 1), 0)
    fin_tile = lambda s: (jnp.clip(s - n_cast - 1, 0, n_tiles - 1), 0)
    in_specs = [
        pl.BlockSpec((OUT_TM, D_GLA), mm_tile),
        pl.BlockSpec((OUT_TM, D_ATT), mm_tile),
        pl.BlockSpec((None, OUT_W_ROWS, D_MODEL), lambda s: (layer, jnp.minimum(s, n_cast - 1), 0)),
        pl.BlockSpec((OUT_TM, D_MODEL), fin_tile),
        pl.BlockSpec((None, 1, D_MODEL), lambda s: (layer, 0, 0)),
    ]
    out_specs = [pl.BlockSpec((OUT_TM, D_MODEL), fin_tile)]
    out_shape = [jax.ShapeDtypeStruct((m, D_MODEL), jnp.float32)]
    args = [og, oa, w_out, x, g_post]
    if fused:
        in_specs.append(pl.BlockSpec((None, 1, D_MODEL), lambda s: (layer + 1, 0, 0)))
        out_specs.append(pl.BlockSpec((OUT_TM, D_MODEL), fin_tile))
        out_shape.append(jax.ShapeDtypeStruct((m, D_MODEL), jnp.bfloat16))
        args.append(g_pre)
    return pl.pallas_call(
        functools.partial(_out_proj_kernel, fused, n_cast, n_tiles),
        grid=(n_cast + n_tiles + 1,),
        in_specs=in_specs,
        out_specs=out_specs,
        out_shape=out_shape,
        scratch_shapes=[pltpu.VMEM((D_GLA + D_ATT, D_MODEL), jnp.bfloat16),
                        pltpu.VMEM((OUT_TM, D_MODEL), jnp.float32),
                        pltpu.VMEM((OUT_TM, D_MODEL), jnp.float32)],
        compiler_params=pltpu.CompilerParams(
            dimension_semantics=("arbitrary",), vmem_limit_bytes=VMEM_LIMIT),
        name="out_proj",
    )(*args)


def kernel(x, w_in, w_out, g_pre, g_post, w_alpha, b_alpha, g_gla, g_att, rel_bias):
    batch, seq, d_model = x.shape
    depth = w_in.shape[0]
    h = x.reshape(batch * seq, d_model)
    w_in_t = jnp.swapaxes(w_in, 1, 2).reshape(depth * D_IN, d_model)
    walt = jnp.pad(jnp.swapaxes(w_alpha, 1, 2), ((0, 0), (0, 0), (0, LANES - GATE_RANK))).astype(jnp.bfloat16)
    bcol = b_alpha[:, :, None]
    bias_rows, band_mask = _band_bias_rows(rel_bias), _band_mask()
    row = lambda p: p[:, None, :]
    hn = _norm_call(0, h, row(g_pre))
    for l in range(depth):
        kt, gat = _proj_nt(l, hn, w_in_t)
        (zg,) = _proj_nn(l, hn, w_in_t, _GLA_ROWS)
        za, og = _proj_nn(l, hn, w_in_t, _ATT_ROWS, gla=(zg, kt, gat, walt, bcol, row(g_gla), batch, seq))
        oa = _band_attn(l, za, kt, bias_rows, band_mask, row(g_att), batch, seq)
        if l + 1 < depth:
            h, hn = _out_proj(l, og, oa, w_out, h, row(g_post), row(g_pre))
        else:
            (h,) = _out_proj(l, og, oa, w_out, h, row(g_post))
    return h.reshape(batch, seq, d_model)
```

```python
import functools
import math

import jax
import jax.numpy as jnp
import numpy as np
from jax import lax
from jax.experimental import pallas as pl
from jax.experimental.pallas import tpu as pltpu

D_MODEL = 2048
CHUNK = 64
D_GLA = 1024
D_ATT = 1024
GLA_HEADS = 4
GLA_DK = 128
GLA_DV = 256
GLA_KW = GLA_HEADS * GLA_DK
GATE_RANK = 16
GLA_TAU = 16.0
ATT_HEADS = 8
ATT_HD = 128
LEFT_CHUNKS = 8
REL_CLIP = 128
EPS = 1e-6
LOG2E = math.log2(math.e)

LANES = 128

W_GA = 2 * GLA_KW + 2 * D_GLA
W_AQ = W_GA + GATE_RANK
D_IN = W_AQ + 4 * D_ATT
ROW_ALIGN = GATE_RANK
PROJ_TN = 512
Z_GQ, Z_GV, Z_GG = 0, GLA_KW, GLA_KW + D_GLA
Z_AQ, Z_AV, Z_AG = 0, D_ATT, 2 * D_ATT
KT_G, KT_A = 0, GLA_KW
D_KT = GLA_KW + D_ATT

PROJ_TM = 2048
MIX_PIECE_ROWS, MIX_PIECE_COLS = 256, 256
NORM_TM = 1024
OUT_TM = 256
GLA_GROUP = 256
ATT_TQ = 128
ATT_WIN = ATT_TQ + LEFT_CHUNKS * CHUNK
CAST_ROWS = 256
VMEM_LIMIT = 48 * 1024 * 1024
VMEM_LIMIT_FUSED = 56 * 1024 * 1024


def _silu(x):
    half = 0.5 * x
    return half + half * jnp.tanh(half)


def _log_sigmoid(x):
    return jnp.minimum(x, 0.0) - jnp.log(1.0 + jnp.exp(-jnp.abs(x)))


def _rms_scale(x):
    return lax.rsqrt(jnp.mean(x * x, axis=-1, keepdims=True) + EPS)


def _norm_kernel(x_ref, gpre_ref, hn_ref):
    x = x_ref[...]
    hn_ref[...] = (x * _rms_scale(x) * gpre_ref[...]).astype(hn_ref.dtype)


def _norm_call(layer, x, g_pre):
    rows = x.shape[0]
    return pl.pallas_call(
        _norm_kernel,
        grid=(rows // NORM_TM,),
        in_specs=[pl.BlockSpec((NORM_TM, D_MODEL), lambda i: (i, 0)),
                  pl.BlockSpec((None, 1, D_MODEL), lambda i: (layer, 0, 0))],
        out_specs=pl.BlockSpec((NORM_TM, D_MODEL), lambda i: (i, 0)),
        out_shape=jax.ShapeDtypeStruct((rows, D_MODEL), jnp.bfloat16),
        compiler_params=pltpu.CompilerParams(
            dimension_semantics=("parallel",), vmem_limit_bytes=VMEM_LIMIT),
        name="norm",
    )(x, g_pre)


def _split3_bf16(x):
    hi = x.astype(jnp.bfloat16)
    r1 = x - hi.astype(jnp.float32)
    mid = r1.astype(jnp.bfloat16)
    lo = (r1 - mid.astype(jnp.float32)).astype(jnp.bfloat16)
    return hi, mid, lo


def _gla_unit(side_work, q_ref, kt_ref, v_ref, gate_ref, gat_ref, walt_ref, bcol_ref, g_ref, o_ref,
              st_ref, parts_ref, sums_ref, u_ref, a_ref):
    seq = q_ref.shape[0]
    n_groups = seq // GLA_GROUP
    chunks_per_group = GLA_GROUP // CHUNK
    r = lax.broadcasted_iota(jnp.int32, (3 * GLA_GROUP, 2 * GLA_GROUP), 0) % GLA_GROUP
    c = lax.broadcasted_iota(jnp.int32, (3 * GLA_GROUP, 2 * GLA_GROUP), 1)
    sums = jnp.where((r // CHUNK == (c % GLA_GROUP) // CHUNK) & (r <= c), 1.0, 0.0).astype(jnp.bfloat16)
    frame_chunk = lax.broadcasted_iota(jnp.int32, (GLA_DK, GLA_GROUP), 1) // CHUNK
    ga_pad = jnp.zeros((LANES - GATE_RANK, GLA_GROUP), jnp.bfloat16)
    g = g_ref[...]

    def gate_step(gi):
        ga = jnp.concatenate([gat_ref[:, pl.ds(gi * GLA_GROUP, GLA_GROUP)], ga_pad], axis=0)
        pre = jnp.dot(walt_ref[...], ga, preferred_element_type=jnp.float32) + bcol_ref[...]
        log_a = _log_sigmoid(pre) * (1.0 / GLA_TAU)
        parts_ref[pl.ds(gi * GLA_DK, GLA_DK), :] = jnp.concatenate(_split3_bf16(log_a), axis=1)

    def sums_step():
        sums_ref[...] = jnp.dot(parts_ref[...], sums, preferred_element_type=jnp.float32)

    def decay_step(gi):
        cols = pl.ds(gi * GLA_GROUP, GLA_GROUP)
        both = sums_ref[pl.ds(gi * GLA_DK, GLA_DK), :]
        run, total = both[:, :GLA_GROUP], both[:, GLA_GROUP:]
        k_dec = (kt_ref[:, cols].astype(jnp.float32) * jnp.exp(total - run)).astype(jnp.bfloat16)
        k_own = jnp.concatenate([jnp.where(frame_chunk == ci, k_dec, jnp.zeros_like(k_dec))
                                 for ci in range(chunks_per_group)], axis=0)
        u = jnp.dot(k_own, v_ref[cols, :], preferred_element_type=jnp.float32)
        a = jnp.exp(total)
        for ci in range(chunks_per_group):
            u_ref[gi, ci] = u[ci * GLA_DK:(ci + 1) * GLA_DK]
            a_ref[gi, ci] = jnp.broadcast_to(a[:, ci * CHUNK:ci * CHUNK + 1], (GLA_DK, GLA_DV))

    def state_step(gi):
        s_c = st_ref[...]
        for ci in range(chunks_per_group):
            s_c = a_ref[gi, ci] * s_c + u_ref[gi, ci]
            rows = pl.ds(gi * GLA_GROUP + ci * CHUNK, CHUNK)
            o = jnp.dot(q_ref[rows, :], s_c.astype(jnp.bfloat16), preferred_element_type=jnp.float32)
            o = o * _rms_scale(o) * g
            o_ref[rows, :] = (o * _silu(gate_ref[rows, :].astype(jnp.float32))).astype(o_ref.dtype)
        st_ref[...] = s_c

    stages = ([functools.partial(gate_step, gi) for gi in range(n_groups)] + [sums_step]
              + [functools.partial(decay_step, gi) for gi in range(n_groups)]
              + [functools.partial(state_step, gi) for gi in range(n_groups)])
    side_at = {(k * len(stages)) // len(side_work): fn for k, fn in enumerate(side_work)}
    st_ref[...] = jnp.zeros_like(st_ref)
    for k, stage in enumerate(stages):
        if k in side_at:
            side_at[k]()
        stage()


def _gla_specs(layer, seq, unit):
    b_h = lambda off: (lambda *idx: (unit(*idx)[0], off + unit(*idx)[1]))
    in_specs = [
        pl.BlockSpec((seq, GLA_DK), b_h(Z_GQ // GLA_DK)),
        pl.BlockSpec((GLA_DK, seq), lambda *idx: (KT_G // GLA_DK + unit(*idx)[1], unit(*idx)[0])),
        pl.BlockSpec((seq, GLA_DV), b_h(Z_GV // GLA_DV)),
        pl.BlockSpec((seq, GLA_DV), b_h(Z_GG // GLA_DV)),
        pl.BlockSpec((None, GATE_RANK, seq), lambda *idx: (0, 0, unit(*idx)[0])),
        pl.BlockSpec((None, GLA_DK, LANES), lambda *idx: (layer, unit(*idx)[1], 0)),
        pl.BlockSpec((None, GLA_DK, 1), lambda *idx: (layer, unit(*idx)[1], 0)),
        pl.BlockSpec((None, 1, GLA_DV), lambda *idx: (layer, 0, unit(*idx)[1])),
    ]
    out_spec = pl.BlockSpec((seq, GLA_DV), lambda *idx: unit(*idx))
    return in_specs, out_spec


def _gla_scratch(seq):
    n_groups = seq // GLA_GROUP
    chunks_per_group = GLA_GROUP // CHUNK
    return [pltpu.VMEM((GLA_DK, GLA_DV), jnp.float32),
            pltpu.VMEM((n_groups * GLA_DK, 3 * GLA_GROUP), jnp.bfloat16),
            pltpu.VMEM((n_groups * GLA_DK, 2 * GLA_GROUP), jnp.float32),
            pltpu.VMEM((n_groups, chunks_per_group, GLA_DK, GLA_DV), jnp.float32),
            pltpu.VMEM((n_groups, chunks_per_group, GLA_DK, GLA_DV), jnp.float32)]


def _attn_scores(q_ref, kt_ref, bias_ref, q_start, k_start, width):
    s = jnp.dot(q_ref[q_start:q_start + ATT_TQ, :], kt_ref[:, k_start:k_start + width],
                preferred_element_type=jnp.float32)
    return s + bias_ref[:, ATT_WIN - width:]


def _attn_softmax(s2):
    m = jnp.max(s2, axis=-1, keepdims=True)
    e = jnp.exp2(s2 - m)
    return e.astype(jnp.bfloat16), jnp.sum(e, axis=-1, keepdims=True)


def _attn_finish(p, denom, v_ref, gate_ref, g, o_ref, q_start, k_start, width):
    o = jnp.dot(p, v_ref[k_start:k_start + width, :], preferred_element_type=jnp.float32)
    o = o * (1.0 / denom)
    o = o * _rms_scale(o) * g
    rows = slice(q_start, q_start + ATT_TQ)
    o_ref[rows, :] = (o * _silu(gate_ref[rows, :].astype(jnp.float32))).astype(o_ref.dtype)


def _attn_kernel(q_ref, kt_ref, v_ref, gate_ref, brow_ref, mask_ref, g_ref, o_ref, bias_ref):
    seq = q_ref.shape[0]
    g = g_ref[...]
    n_tiles = seq // ATT_TQ

    @pl.when(pl.program_id(1) == 0)
    def _():
        rows = jnp.broadcast_to(brow_ref[...], (ATT_TQ, ATT_TQ + ATT_WIN))
        bias_ref[...] = pltpu.roll(rows, 0, 1, stride=1, stride_axis=0)[:, :ATT_WIN] + mask_ref[...]

    def window(t):
        k_start = max(0, (t + 1) * ATT_TQ - ATT_WIN)
        return t * ATT_TQ, k_start, (t + 1) * ATT_TQ - k_start

    scores, probs = {}, {}
    for step in range(n_tiles + 2):
        if step < n_tiles:
            scores[step] = _attn_scores(q_ref, kt_ref, bias_ref, *window(step))
        if 0 <= step - 1 < n_tiles:
            probs[step - 1] = _attn_softmax(scores.pop(step - 1))
        if 0 <= step - 2:
            _attn_finish(*probs.pop(step - 2), v_ref, gate_ref, g, o_ref, *window(step - 2))


def _band_attn(layer, z, kt, bias_rows, band_mask, g_att, batch, seq):
    blk = lambda off: off // ATT_HD
    b_h = lambda off: (lambda h, b: (b, off + h))
    return pl.pallas_call(
        _attn_kernel,
        grid=(ATT_HEADS, batch),
        in_specs=[
            pl.BlockSpec((seq, ATT_HD), b_h(blk(Z_AQ))),
            pl.BlockSpec((ATT_HD, seq), lambda h, b: (blk(KT_A) + h, b)),
            pl.BlockSpec((seq, ATT_HD), b_h(blk(Z_AV))),
            pl.BlockSpec((seq, ATT_HD), b_h(blk(Z_AG))),
            pl.BlockSpec((None, None, 1, ATT_TQ + ATT_WIN), lambda h, b: (layer, h, 0, 0)),
            pl.BlockSpec((ATT_TQ, ATT_WIN), lambda h, b: (0, 0)),
            pl.BlockSpec((None, 1, ATT_HD), lambda h, b: (layer, 0, h)),
        ],
        out_specs=pl.BlockSpec((seq, ATT_HD), lambda h, b: (b, h)),
        out_shape=jax.ShapeDtypeStruct((z.shape[0], D_ATT), jnp.bfloat16),
        scratch_shapes=[pltpu.VMEM((ATT_TQ, ATT_WIN), jnp.float32)],
        compiler_params=pltpu.CompilerParams(
            dimension_semantics=("arbitrary", "arbitrary"), vmem_limit_bytes=VMEM_LIMIT),
        name="band_attn",
    )(z, kt, z, z, bias_rows, band_mask, g_att)


def _band_bias_rows(rel_bias):
    lead = rel_bias.shape[:-1]
    rb = rel_bias.astype(jnp.float32) * LOG2E
    far = rb[..., 2 * REL_CLIP:]
    row = jnp.concatenate([jnp.broadcast_to(far, lead + (ATT_WIN - REL_CLIP - (ATT_TQ - 1),)),
                           rb[..., 2 * REL_CLIP - 1:0:-1],
                           jnp.zeros(lead + (1,), jnp.float32),
                           jnp.broadcast_to(far, lead + (ATT_TQ - 1,))], axis=-1)
    return row[..., None, :]


def _band_mask():
    i = np.arange(ATT_TQ)[:, None]
    j = np.arange(ATT_WIN)[None, :]
    first = (i // CHUNK) * CHUNK
    in_band = (j >= first) & (j < first + (LEFT_CHUNKS + 1) * CHUNK)
    return jnp.asarray(np.where(in_band, 0.0, -1e30), jnp.float32)


_GLA_ROWS = [0] + [2 * GLA_KW + t * PROJ_TN for t in range(2 * D_GLA // PROJ_TN)]
_ATT_ROWS = ([W_AQ + t * PROJ_TN for t in range(D_ATT // PROJ_TN)]
             + [W_AQ + 2 * D_ATT + t * PROJ_TN for t in range(2 * D_ATT // PROJ_TN)])
_NT_ROWS = [GLA_KW] + [W_AQ + D_ATT + t * PROJ_TN for t in range(D_ATT // PROJ_TN)]


def _table_lookup(table, j):
    val = table[0] + j * 0
    for t in range(1, len(table)):
        val = jnp.where(j >= t, table[t], val)
    return val


def _cast_rows(src_ref, dst_ref, n_rows):
    def body(r, carry):
        rows = pl.ds(pl.multiple_of(r * CAST_ROWS, CAST_ROWS), CAST_ROWS)
        dst_ref[rows, :] = src_ref[rows, :].astype(dst_ref.dtype)
        return carry
    lax.fori_loop(0, n_rows // CAST_ROWS, body, 0)


def _w_rows_spec(layer, table, n_rows, tiles):
    units = [(layer * D_IN + r) // ROW_ALIGN for r in table]
    return pl.BlockSpec((pl.Element(n_rows), pl.Element(D_MODEL)),
                        lambda s: (_table_lookup(units, s // tiles) * ROW_ALIGN, 0))


def _proj_nn_kernel(tiles, n_units, hn_ref, wt_ref, *refs):
    if n_units:
        gla_in, (z_ref, og_ref, wbf_ref), gla_scratch = refs[:8], refs[8:11], refs[11:]
    else:
        z_ref, wbf_ref = refs
    s = pl.program_id(0)

    @pl.when(s % tiles == 0)
    def _():
        _cast_rows(wt_ref, wbf_ref, PROJ_TN)

    def matmul_piece(r0, n_rows, c0, n_cols):
        def run():
            rows, cols = slice(r0, r0 + n_rows), slice(c0, c0 + n_cols)
            z_ref[rows, cols] = lax.dot_general(hn_ref[rows, :], wbf_ref[cols, :], (((1,), (1,)), ((), ())),
                                                preferred_element_type=jnp.float32).astype(z_ref.dtype)
        return run

    matmul = matmul_piece(0, PROJ_TM, 0, PROJ_TN)

    if n_units:
        @pl.when(s < n_units)
        def _():
            pieces = [matmul_piece(r0, MIX_PIECE_ROWS, c0, MIX_PIECE_COLS)
                      for r0 in range(0, PROJ_TM, MIX_PIECE_ROWS) for c0 in range(0, PROJ_TN, MIX_PIECE_COLS)]
            _gla_unit(pieces, *gla_in, og_ref, *gla_scratch)

        pl.when(s >= n_units)(matmul)
    else:
        matmul()


def _proj_nn(layer, hn, w_in_t, table, gla=None):
    m = hn.shape[0]
    tiles = m // PROJ_TM
    steps = len(table) * tiles
    in_specs = [pl.BlockSpec((PROJ_TM, D_MODEL), lambda s: (s % tiles, 0)),
                _w_rows_spec(layer, table, PROJ_TN, tiles)]
    out_specs = [pl.BlockSpec((PROJ_TM, PROJ_TN), lambda s: (s % tiles, s // tiles))]
    out_shape = [jax.ShapeDtypeStruct((m, len(table) * PROJ_TN), jnp.bfloat16)]
    scratch = [pltpu.VMEM((PROJ_TN, D_MODEL), jnp.bfloat16)]
    args = [hn, w_in_t]
    n_units = 0
    if gla is not None:
        zg, kt, gat, walt, bcol, g_gla, batch, seq = gla
        n_units = batch * GLA_HEADS
        assert n_units <= steps

        def unit(s):
            u = jnp.minimum(s, n_units - 1)
            return u // GLA_HEADS, u % GLA_HEADS

        g_in, g_out = _gla_specs(layer, seq, unit)
        in_specs += g_in
        out_specs.append(g_out)
        out_shape.append(jax.ShapeDtypeStruct((m, D_GLA), jnp.bfloat16))
        scratch += _gla_scratch(seq)
        args += [zg, kt, zg, zg, gat, walt, bcol, g_gla]
    return pl.pallas_call(
        functools.partial(_proj_nn_kernel, tiles, n_units),
        grid=(steps,),
        in_specs=in_specs,
        out_specs=out_specs,
        out_shape=out_shape,
        scratch_shapes=scratch,
        compiler_params=pltpu.CompilerParams(
            dimension_semantics=("arbitrary",),
            vmem_limit_bytes=VMEM_LIMIT_FUSED if n_units else VMEM_LIMIT),
        name="proj_att_gla" if n_units else "proj_gla",
    )(*args)


def _proj_nt_kernel(tiles, hn_ref, wt_ref, wga_ref, kt_ref, gat_ref, wbf_ref):
    s = pl.program_id(0)

    @pl.when(s % tiles == 0)
    def _():
        _cast_rows(wt_ref, wbf_ref, PROJ_TN)

    @pl.when(s == 0)
    def _():
        wbf_ref[PROJ_TN:, :] = wga_ref[...].astype(wbf_ref.dtype)

    res = lax.dot_general(wbf_ref[...], hn_ref[...], (((1,), (1,)), ((), ())),
                          preferred_element_type=jnp.float32)
    scale = jnp.where(s < tiles, GLA_DK ** -0.5, ATT_HD ** -0.5 * LOG2E)
    kt_ref[...] = (res[:PROJ_TN] * scale).astype(kt_ref.dtype)
    gat_ref[...] = res[PROJ_TN:].astype(gat_ref.dtype)


def _proj_nt(layer, hn, w_in_t):
    m = hn.shape[0]
    tiles = m // PROJ_TM
    return pl.pallas_call(
        functools.partial(_proj_nt_kernel, tiles),
        grid=(len(_NT_ROWS) * tiles,),
        in_specs=[pl.BlockSpec((PROJ_TM, D_MODEL), lambda s: (s % tiles, 0)),
                  _w_rows_spec(layer, _NT_ROWS, PROJ_TN, tiles),
                  _w_rows_spec(layer, [W_GA], GATE_RANK, len(_NT_ROWS) * tiles)],
        out_specs=[pl.BlockSpec((PROJ_TN, PROJ_TM), lambda s: (s // tiles, s % tiles)),
                   pl.BlockSpec((None, GATE_RANK, PROJ_TM), lambda s: (s // tiles, 0, s % tiles))],
        out_shape=[jax.ShapeDtypeStruct((D_KT, m), jnp.bfloat16),
                   jax.ShapeDtypeStruct((len(_NT_ROWS), GATE_RANK, m), jnp.bfloat16)],
        scratch_shapes=[pltpu.VMEM((PROJ_TN + GATE_RANK, D_MODEL), jnp.bfloat16)],
        compiler_params=pltpu.CompilerParams(
            dimension_semantics=("arbitrary",), vmem_limit_bytes=VMEM_LIMIT),
        name="proj_nt",
    )(hn, w_in_t, w_in_t)


def _out_proj_kernel(fused, n_tiles, og_ref, oa_ref, w_ref, x_ref, gpost_ref, *rest):
    if fused:
        gpre_ref, h_ref, hn_ref, wbf_ref, *y_refs = rest
    else:
        h_ref, wbf_ref, *y_refs = rest
    i = pl.program_id(0)

    @pl.when(i == 0)
    def _():
        _cast_rows(w_ref, wbf_ref, D_GLA + D_ATT)

    def matmul(y_ref):
        y_ref[...] = (jnp.dot(og_ref[...], wbf_ref[:D_GLA, :], preferred_element_type=jnp.float32)
                      + jnp.dot(oa_ref[...], wbf_ref[D_GLA:, :], preferred_element_type=jnp.float32))

    def finish(y_ref):
        y = y_ref[...]
        h = x_ref[...] + y * _rms_scale(y) * gpost_ref[...]
        if fused:
            hn_ref[...] = (h * _rms_scale(h) * gpre_ref[...]).astype(hn_ref.dtype)
        h_ref[...] = h

    pl.when(i == 0)(lambda: matmul(y_refs[0]))
    for parity in range(2):
        @pl.when((i > 0) & (i < n_tiles) & (i % 2 == parity))
        def _():
            matmul(y_refs[parity])
            finish(y_refs[1 - parity])
    pl.when(i == n_tiles)(lambda: finish(y_refs[(n_tiles - 1) % 2]))


def _out_proj(layer, og, oa, w_out, x, g_post, g_pre=None):
    m = x.shape[0]
    fused = g_pre is not None
    n_tiles = m // OUT_TM
    mm_tile = lambda i: (jnp.minimum(i, n_tiles - 1), 0)
    fin_tile = lambda i: (jnp.maximum(i - 1, 0), 0)
    in_specs = [
        pl.BlockSpec((OUT_TM, D_GLA), mm_tile),
        pl.BlockSpec((OUT_TM, D_ATT), mm_tile),
        pl.BlockSpec((None, D_GLA + D_ATT, D_MODEL), lambda i: (layer, 0, 0), pipeline_mode=pl.Buffered(1)),
        pl.BlockSpec((OUT_TM, D_MODEL), fin_tile),
        pl.BlockSpec((None, 1, D_MODEL), lambda i: (layer, 0, 0)),
    ]
    out_specs = [pl.BlockSpec((OUT_TM, D_MODEL), fin_tile)]
    out_shape = [jax.ShapeDtypeStruct((m, D_MODEL), jnp.float32)]
    args = [og, oa, w_out, x, g_post]
    if fused:
        in_specs.append(pl.BlockSpec((None, 1, D_MODEL), lambda i: (layer + 1, 0, 0)))
        out_specs.append(pl.BlockSpec((OUT_TM, D_MODEL), fin_tile))
        out_shape.append(jax.ShapeDtypeStruct((m, D_MODEL), jnp.bfloat16))
        args.append(g_pre)
    return pl.pallas_call(
        functools.partial(_out_proj_kernel, fused, n_tiles),
        grid=(n_tiles + 1,),
        in_specs=in_specs,
        out_specs=out_specs,
        out_shape=out_shape,
        scratch_shapes=[pltpu.VMEM((D_GLA + D_ATT, D_MODEL), jnp.bfloat16),
                        pltpu.VMEM((OUT_TM, D_MODEL), jnp.float32),
                        pltpu.VMEM((OUT_TM, D_MODEL), jnp.float32)],
        compiler_params=pltpu.CompilerParams(
            dimension_semantics=("arbitrary",), vmem_limit_bytes=VMEM_LIMIT),
        name="out_proj",
    )(*args)


def kernel(x, w_in, w_out, g_pre, g_post, w_alpha, b_alpha, g_gla, g_att, rel_bias):
    batch, seq, d_model = x.shape
    depth = w_in.shape[0]
    h = x.reshape(batch * seq, d_model)
    w_in_t = jnp.swapaxes(w_in, 1, 2).reshape(depth * D_IN, d_model)
    walt = jnp.pad(jnp.swapaxes(w_alpha, 1, 2), ((0, 0), (0, 0), (0, LANES - GATE_RANK))).astype(jnp.bfloat16)
    bcol = b_alpha[:, :, None]
    bias_rows, band_mask = _band_bias_rows(rel_bias), _band_mask()
    row = lambda p: p[:, None, :]
    hn = _norm_call(0, h, row(g_pre))
    for l in range(depth):
        kt, gat = _proj_nt(l, hn, w_in_t)
        (zg,) = _proj_nn(l, hn, w_in_t, _GLA_ROWS)
        za, og = _proj_nn(l, hn, w_in_t, _ATT_ROWS, gla=(zg, kt, gat, walt, bcol, row(g_gla), batch, seq))
        oa = _band_attn(l, za, kt, bias_rows, band_mask, row(g_att), batch, seq)
        if l + 1 < depth:
            h, hn = _out_proj(l, og, oa, w_out, h, row(g_post), row(g_pre))
        else:
            (h,) = _out_proj(l, og, oa, w_out, h, row(g_post))
    return h.reshape(batch, seq, d_model)
```

```python
import functools
import math

import jax
import jax.numpy as jnp
import numpy as np
from jax import lax
from jax.experimental import pallas as pl
from jax.experimental.pallas import tpu as pltpu

D_MODEL = 2048
CHUNK = 64
D_GLA = 1024
D_ATT = 1024
GLA_HEADS = 4
GLA_DK = 128
GLA_DV = 256
GLA_KW = GLA_HEADS * GLA_DK
GATE_RANK = 16
GLA_TAU = 16.0
ATT_HEADS = 8
ATT_HD = 128
LEFT_CHUNKS = 8
REL_CLIP = 128
EPS = 1e-6
LOG2E = math.log2(math.e)

LANES = 128

W_GA = 2 * GLA_KW + 2 * D_GLA
W_AQ = W_GA + GATE_RANK
D_IN = W_AQ + 4 * D_ATT
ROW_ALIGN = GATE_RANK
PROJ_TN = 512
Z_GQ, Z_GV, Z_GG = 0, GLA_KW, GLA_KW + D_GLA
Z_AQ, Z_AV, Z_AG = 0, D_ATT, 2 * D_ATT
KT_G, KT_A = 0, GLA_KW
D_KT = GLA_KW + D_ATT

PROJ_TM = 2048
MIX_PIECE_ROWS, MIX_PIECE_COLS = 256, 256
NORM_TM = 1024
OUT_TM = 256
GLA_GROUP = 256
ATT_TQ = 128
ATT_HEADS_PER_STEP = 2
ATT_WIN = ATT_TQ + LEFT_CHUNKS * CHUNK
CAST_ROWS = 256
VMEM_LIMIT = 48 * 1024 * 1024
VMEM_LIMIT_FUSED = 56 * 1024 * 1024


def _silu(x):
    half = 0.5 * x
    return half + half * jnp.tanh(half)


def _log_sigmoid(x):
    return jnp.minimum(x, 0.0) - jnp.log(1.0 + jnp.exp(-jnp.abs(x)))


def _rms_scale(x):
    return lax.rsqrt(jnp.mean(x * x, axis=-1, keepdims=True) + EPS)


def _norm_kernel(x_ref, gpre_ref, hn_ref):
    x = x_ref[...]
    hn_ref[...] = (x * _rms_scale(x) * gpre_ref[...]).astype(hn_ref.dtype)


def _norm_call(layer, x, g_pre):
    rows = x.shape[0]
    return pl.pallas_call(
        _norm_kernel,
        grid=(rows // NORM_TM,),
        in_specs=[pl.BlockSpec((NORM_TM, D_MODEL), lambda i: (i, 0)),
                  pl.BlockSpec((None, 1, D_MODEL), lambda i: (layer, 0, 0))],
        out_specs=pl.BlockSpec((NORM_TM, D_MODEL), lambda i: (i, 0)),
        out_shape=jax.ShapeDtypeStruct((rows, D_MODEL), jnp.bfloat16),
        compiler_params=pltpu.CompilerParams(
            dimension_semantics=("parallel",), vmem_limit_bytes=VMEM_LIMIT),
        name="norm",
    )(x, g_pre)


def _split3_bf16(x):
    hi = x.astype(jnp.bfloat16)
    r1 = x - hi.astype(jnp.float32)
    mid = r1.astype(jnp.bfloat16)
    lo = (r1 - mid.astype(jnp.float32)).astype(jnp.bfloat16)
    return hi, mid, lo


def _gla_unit(side_work, q_ref, kt_ref, v_ref, gate_ref, gat_ref, walt_ref, bcol_ref, g_ref, o_ref,
              st_ref, parts_ref, sums_ref, u_ref, a_ref):
    seq = q_ref.shape[0]
    n_groups = seq // GLA_GROUP
    chunks_per_group = GLA_GROUP // CHUNK
    r = lax.broadcasted_iota(jnp.int32, (3 * GLA_GROUP, 2 * GLA_GROUP), 0) % GLA_GROUP
    c = lax.broadcasted_iota(jnp.int32, (3 * GLA_GROUP, 2 * GLA_GROUP), 1)
    sums = jnp.where((r // CHUNK == (c % GLA_GROUP) // CHUNK) & (r <= c), 1.0, 0.0).astype(jnp.bfloat16)
    frame_chunk = lax.broadcasted_iota(jnp.int32, (GLA_DK, GLA_GROUP), 1) // CHUNK
    ga_pad = jnp.zeros((LANES - GATE_RANK, GLA_GROUP), jnp.bfloat16)
    g = g_ref[...]

    def gate_step(gi):
        ga = jnp.concatenate([gat_ref[:, pl.ds(gi * GLA_GROUP, GLA_GROUP)], ga_pad], axis=0)
        pre = jnp.dot(walt_ref[...], ga, preferred_element_type=jnp.float32) + bcol_ref[...]
        log_a = _log_sigmoid(pre) * (1.0 / GLA_TAU)
        parts_ref[pl.ds(gi * GLA_DK, GLA_DK), :] = jnp.concatenate(_split3_bf16(log_a), axis=1)

    def sums_step():
        sums_ref[...] = jnp.dot(parts_ref[...], sums, preferred_element_type=jnp.float32)

    def decay_step(gi):
        cols = pl.ds(gi * GLA_GROUP, GLA_GROUP)
        both = sums_ref[pl.ds(gi * GLA_DK, GLA_DK), :]
        run, total = both[:, :GLA_GROUP], both[:, GLA_GROUP:]
        k_dec = (kt_ref[:, cols].astype(jnp.float32) * jnp.exp(total - run)).astype(jnp.bfloat16)
        k_own = jnp.concatenate([jnp.where(frame_chunk == ci, k_dec, jnp.zeros_like(k_dec))
                                 for ci in range(chunks_per_group)], axis=0)
        u = jnp.dot(k_own, v_ref[cols, :], preferred_element_type=jnp.float32)
        a = jnp.exp(total)
        for ci in range(chunks_per_group):
            u_ref[gi, ci] = u[ci * GLA_DK:(ci + 1) * GLA_DK]
            a_ref[gi, ci] = jnp.broadcast_to(a[:, ci * CHUNK:ci * CHUNK + 1], (GLA_DK, GLA_DV))

    def state_step(gi):
        s_c = st_ref[...]
        for ci in range(chunks_per_group):
            s_c = a_ref[gi, ci] * s_c + u_ref[gi, ci]
            rows = pl.ds(gi * GLA_GROUP + ci * CHUNK, CHUNK)
            o = jnp.dot(q_ref[rows, :], s_c.astype(jnp.bfloat16), preferred_element_type=jnp.float32)
            o = o * _rms_scale(o) * g
            o_ref[rows, :] = (o * _silu(gate_ref[rows, :].astype(jnp.float32))).astype(o_ref.dtype)
        st_ref[...] = s_c

    stages = ([functools.partial(gate_step, gi) for gi in range(n_groups)] + [sums_step]
              + [functools.partial(decay_step, gi) for gi in range(n_groups)]
              + [functools.partial(state_step, gi) for gi in range(n_groups)])
    side_at = {(k * len(stages)) // len(side_work): fn for k, fn in enumerate(side_work)}
    st_ref[...] = jnp.zeros_like(st_ref)
    for k, stage in enumerate(stages):
        if k in side_at:
            side_at[k]()
        stage()


def _gla_specs(layer, seq, unit):
    b_h = lambda off: (lambda *idx: (unit(*idx)[0], off + unit(*idx)[1]))
    in_specs = [
        pl.BlockSpec((seq, GLA_DK), b_h(Z_GQ // GLA_DK)),
        pl.BlockSpec((GLA_DK, seq), lambda *idx: (KT_G // GLA_DK + unit(*idx)[1], unit(*idx)[0])),
        pl.BlockSpec((seq, GLA_DV), b_h(Z_GV // GLA_DV)),
        pl.BlockSpec((seq, GLA_DV), b_h(Z_GG // GLA_DV)),
        pl.BlockSpec((None, GATE_RANK, seq), lambda *idx: (0, 0, unit(*idx)[0])),
        pl.BlockSpec((None, GLA_DK, LANES), lambda *idx: (layer, unit(*idx)[1], 0)),
        pl.BlockSpec((None, GLA_DK, 1), lambda *idx: (layer, unit(*idx)[1], 0)),
        pl.BlockSpec((None, 1, GLA_DV), lambda *idx: (layer, 0, unit(*idx)[1])),
    ]
    out_spec = pl.BlockSpec((seq, GLA_DV), lambda *idx: unit(*idx))
    return in_specs, out_spec


def _gla_scratch(seq):
    n_groups = seq // GLA_GROUP
    chunks_per_group = GLA_GROUP // CHUNK
    return [pltpu.VMEM((GLA_DK, GLA_DV), jnp.float32),
            pltpu.VMEM((n_groups * GLA_DK, 3 * GLA_GROUP), jnp.bfloat16),
            pltpu.VMEM((n_groups * GLA_DK, 2 * GLA_GROUP), jnp.float32),
            pltpu.VMEM((n_groups, chunks_per_group, GLA_DK, GLA_DV), jnp.float32),
            pltpu.VMEM((n_groups, chunks_per_group, GLA_DK, GLA_DV), jnp.float32)]


def _attn_scores(q_ref, kt_ref, bias_ref, q_start, k_start, width):
    s = jnp.dot(q_ref[q_start:q_start + ATT_TQ, :], kt_ref[:, k_start:k_start + width],
                preferred_element_type=jnp.float32)
    return s + bias_ref[:, ATT_WIN - width:]


def _attn_softmax(s2):
    m = jnp.max(s2, axis=-1, keepdims=True)
    e = jnp.exp2(s2 - m)
    return e.astype(jnp.bfloat16), jnp.sum(e, axis=-1, keepdims=True)


def _attn_finish(p, denom, v_ref, gate_ref, g, o_ref, q_start, k_start, width):
    o = jnp.dot(p, v_ref[k_start:k_start + width, :], preferred_element_type=jnp.float32)
    o = o * (1.0 / denom)
    o = o * _rms_scale(o) * g
    rows = slice(q_start, q_start + ATT_TQ)
    o_ref[rows, :] = (o * _silu(gate_ref[rows, :].astype(jnp.float32))).astype(o_ref.dtype)


def _attn_kernel(q_ref, kt_ref, v_ref, gate_ref, brow_ref, mask_ref, g_ref, o_ref, bias_ref):
    seq = q_ref.shape[0]
    n_tiles = seq // ATT_TQ
    heads = range(ATT_HEADS_PER_STEP)
    cols = [pl.ds(j * ATT_HD, ATT_HD) for j in heads]
    g = [g_ref[:, cols[j]] for j in heads]

    @pl.when(pl.program_id(1) == 0)
    def _():
        for j in heads:
            rows = jnp.broadcast_to(brow_ref[j], (ATT_TQ, ATT_TQ + ATT_WIN))
            bias_ref[j] = pltpu.roll(rows, 0, 1, stride=1, stride_axis=0)[:, :ATT_WIN] + mask_ref[...]

    def window(t):
        k_start = max(0, (t + 1) * ATT_TQ - ATT_WIN)
        return t * ATT_TQ, k_start, (t + 1) * ATT_TQ - k_start

    scores, probs = {}, {}
    for step in range(n_tiles + 2):
        for j in heads:
            if step < n_tiles:
                scores[step, j] = _attn_scores(q_ref.at[:, cols[j]], kt_ref.at[cols[j], :], bias_ref.at[j], *window(step))
            if 0 <= step - 1 < n_tiles:
                probs[step - 1, j] = _attn_softmax(scores.pop((step - 1, j)))
            if 0 <= step - 2:
                _attn_finish(*probs.pop((step - 2, j)), v_ref.at[:, cols[j]], gate_ref.at[:, cols[j]], g[j],
                             o_ref.at[:, cols[j]], *window(step - 2))


def _band_attn(layer, z, kt, bias_rows, band_mask, g_att, batch, seq):
    width = ATT_HEADS_PER_STEP * ATT_HD
    blk = lambda off: off // width
    b_h = lambda off: (lambda h, b: (b, off + h))
    return pl.pallas_call(
        _attn_kernel,
        grid=(ATT_HEADS // ATT_HEADS_PER_STEP, batch),
        in_specs=[
            pl.BlockSpec((seq, width), b_h(blk(Z_AQ))),
            pl.BlockSpec((width, seq), lambda h, b: (blk(KT_A) + h, b)),
            pl.BlockSpec((seq, width), b_h(blk(Z_AV))),
            pl.BlockSpec((seq, width), b_h(blk(Z_AG))),
            pl.BlockSpec((None, ATT_HEADS_PER_STEP, 1, ATT_TQ + ATT_WIN), lambda h, b: (layer, h, 0, 0)),
            pl.BlockSpec((ATT_TQ, ATT_WIN), lambda h, b: (0, 0)),
            pl.BlockSpec((None, 1, width), lambda h, b: (layer, 0, h)),
        ],
        out_specs=pl.BlockSpec((seq, width), lambda h, b: (b, h)),
        out_shape=jax.ShapeDtypeStruct((z.shape[0], D_ATT), jnp.bfloat16),
        scratch_shapes=[pltpu.VMEM((ATT_HEADS_PER_STEP, ATT_TQ, ATT_WIN), jnp.float32)],
        compiler_params=pltpu.CompilerParams(
            dimension_semantics=("arbitrary", "arbitrary"), vmem_limit_bytes=VMEM_LIMIT),
        name="band_attn",
    )(z, kt, z, z, bias_rows, band_mask, g_att)


def _band_bias_rows(rel_bias):
    lead = rel_bias.shape[:-1]
    rb = rel_bias.astype(jnp.float32) * LOG2E
    far = rb[..., 2 * REL_CLIP:]
    row = jnp.concatenate([jnp.broadcast_to(far, lead + (ATT_WIN - REL_CLIP - (ATT_TQ - 1),)),
                           rb[..., 2 * REL_CLIP - 1:0:-1],
                           jnp.zeros(lead + (1,), jnp.float32),
                           jnp.broadcast_to(far, lead + (ATT_TQ - 1,))], axis=-1)
    return row[..., None, :]


def _band_mask():
    i = np.arange(ATT_TQ)[:, None]
    j = np.arange(ATT_WIN)[None, :]
    first = (i // CHUNK) * CHUNK
    in_band = (j >= first) & (j < first + (LEFT_CHUNKS + 1) * CHUNK)
    return jnp.asarray(np.where(in_band, 0.0, -1e30), jnp.float32)


_GLA_ROWS = [0] + [2 * GLA_KW + t * PROJ_TN for t in range(2 * D_GLA // PROJ_TN)]
_ATT_ROWS = ([W_AQ + t * PROJ_TN for t in range(D_ATT // PROJ_TN)]
             + [W_AQ + 2 * D_ATT + t * PROJ_TN for t in range(2 * D_ATT // PROJ_TN)])
_NT_ROWS = [GLA_KW] + [W_AQ + D_ATT + t * PROJ_TN for t in range(D_ATT // PROJ_TN)]


def _table_lookup(table, j):
    val = table[0] + j * 0
    for t in range(1, len(table)):
        val = jnp.where(j >= t, table[t], val)
    return val


def _cast_rows(src_ref, dst_ref, n_rows):
    def body(r, carry):
        rows = pl.ds(pl.multiple_of(r * CAST_ROWS, CAST_ROWS), CAST_ROWS)
        dst_ref[rows, :] = src_ref[rows, :].astype(dst_ref.dtype)
        return carry
    lax.fori_loop(0, n_rows // CAST_ROWS, body, 0)


def _w_rows_spec(layer, table, n_rows, tiles):
    units = [(layer * D_IN + r) // ROW_ALIGN for r in table]
    return pl.BlockSpec((pl.Element(n_rows), pl.Element(D_MODEL)),
                        lambda s: (_table_lookup(units, s // tiles) * ROW_ALIGN, 0))


def _proj_nn_kernel(tiles, n_units, hn_ref, wt_ref, *refs):
    if n_units:
        gla_in, (z_ref, og_ref, wbf_ref), gla_scratch = refs[:8], refs[8:11], refs[11:]
    else:
        z_ref, wbf_ref = refs
    s = pl.program_id(0)

    @pl.when(s % tiles == 0)
    def _():
        _cast_rows(wt_ref, wbf_ref, PROJ_TN)

    def matmul_piece(r0, n_rows, c0, n_cols):
        def run():
            rows, cols = slice(r0, r0 + n_rows), slice(c0, c0 + n_cols)
            z_ref[rows, cols] = lax.dot_general(hn_ref[rows, :], wbf_ref[cols, :], (((1,), (1,)), ((), ())),
                                                preferred_element_type=jnp.float32).astype(z_ref.dtype)
        return run

    matmul = matmul_piece(0, PROJ_TM, 0, PROJ_TN)

    if n_units:
        @pl.when(s < n_units)
        def _():
            pieces = [matmul_piece(r0, MIX_PIECE_ROWS, c0, MIX_PIECE_COLS)
                      for r0 in range(0, PROJ_TM, MIX_PIECE_ROWS) for c0 in range(0, PROJ_TN, MIX_PIECE_COLS)]
            _gla_unit(pieces, *gla_in, og_ref, *gla_scratch)

        pl.when(s >= n_units)(matmul)
    else:
        matmul()


def _proj_nn(layer, hn, w_in_t, table, gla=None):
    m = hn.shape[0]
    tiles = m // PROJ_TM
    steps = len(table) * tiles
    in_specs = [pl.BlockSpec((PROJ_TM, D_MODEL), lambda s: (s % tiles, 0)),
                _w_rows_spec(layer, table, PROJ_TN, tiles)]
    out_specs = [pl.BlockSpec((PROJ_TM, PROJ_TN), lambda s: (s % tiles, s // tiles))]
    out_shape = [jax.ShapeDtypeStruct((m, len(table) * PROJ_TN), jnp.bfloat16)]
    scratch = [pltpu.VMEM((PROJ_TN, D_MODEL), jnp.bfloat16)]
    args = [hn, w_in_t]
    n_units = 0
    if gla is not None:
        zg, kt, gat, walt, bcol, g_gla, batch, seq = gla
        n_units = batch * GLA_HEADS
        assert n_units <= steps

        def unit(s):
            u = jnp.minimum(s, n_units - 1)
            return u // GLA_HEADS, u % GLA_HEADS

        g_in, g_out = _gla_specs(layer, seq, unit)
        in_specs += g_in
        out_specs.append(g_out)
        out_shape.append(jax.ShapeDtypeStruct((m, D_GLA), jnp.bfloat16))
        scratch += _gla_scratch(seq)
        args += [zg, kt, zg, zg, gat, walt, bcol, g_gla]
    return pl.pallas_call(
        functools.partial(_proj_nn_kernel, tiles, n_units),
        grid=(steps,),
        in_specs=in_specs,
        out_specs=out_specs,
        out_shape=out_shape,
        scratch_shapes=scratch,
        compiler_params=pltpu.CompilerParams(
            dimension_semantics=("arbitrary",),
            vmem_limit_bytes=VMEM_LIMIT_FUSED if n_units else VMEM_LIMIT),
        name="proj_att_gla" if n_units else "proj_gla",
    )(*args)


def _proj_nt_kernel(tiles, hn_ref, wt_ref, wga_ref, kt_ref, gat_ref, wbf_ref):
    s = pl.program_id(0)

    @pl.when(s % tiles == 0)
    def _():
        _cast_rows(wt_ref, wbf_ref, PROJ_TN)

    @pl.when(s == 0)
    def _():
        wbf_ref[PROJ_TN:, :] = wga_ref[...].astype(wbf_ref.dtype)

    res = lax.dot_general(wbf_ref[...], hn_ref[...], (((1,), (1,)), ((), ())),
                          preferred_element_type=jnp.float32)
    scale = jnp.where(s < tiles, GLA_DK ** -0.5, ATT_HD ** -0.5 * LOG2E)
    kt_ref[...] = (res[:PROJ_TN] * scale).astype(kt_ref.dtype)
    gat_ref[...] = res[PROJ_TN:].astype(gat_ref.dtype)


def _proj_nt(layer, hn, w_in_t):
    m = hn.shape[0]
    tiles = m // PROJ_TM
    return pl.pallas_call(
        functools.partial(_proj_nt_kernel, tiles),
        grid=(len(_NT_ROWS) * tiles,),
        in_specs=[pl.BlockSpec((PROJ_TM, D_MODEL), lambda s: (s % tiles, 0)),
                  _w_rows_spec(layer, _NT_ROWS, PROJ_TN, tiles),
                  _w_rows_spec(layer, [W_GA], GATE_RANK, len(_NT_ROWS) * tiles)],
        out_specs=[pl.BlockSpec((PROJ_TN, PROJ_TM), lambda s: (s // tiles, s % tiles)),
                   pl.BlockSpec((None, GATE_RANK, PROJ_TM), lambda s: (s // tiles, 0, s % tiles))],
        out_shape=[jax.ShapeDtypeStruct((D_KT, m), jnp.bfloat16),
                   jax.ShapeDtypeStruct((len(_NT_ROWS), GATE_RANK, m), jnp.bfloat16)],
        scratch_shapes=[pltpu.VMEM((PROJ_TN + GATE_RANK, D_MODEL), jnp.bfloat16)],
        compiler_params=pltpu.CompilerParams(
            dimension_semantics=("arbitrary",), vmem_limit_bytes=VMEM_LIMIT),
        name="proj_nt",
    )(hn, w_in_t, w_in_t)


def _out_proj_kernel(fused, n_tiles, og_ref, oa_ref, w_ref, x_ref, gpost_ref, *rest):
    if fused:
        gpre_ref, h_ref, hn_ref, wbf_ref, *y_refs = rest
    else:
        h_ref, wbf_ref, *y_refs = rest
    i = pl.program_id(0)

    @pl.when(i == 0)
    def _():
        _cast_rows(w_ref, wbf_ref, D_GLA + D_ATT)

    def matmul(y_ref):
        y_ref[...] = (jnp.dot(og_ref[...], wbf_ref[:D_GLA, :], preferred_element_type=jnp.float32)
                      + jnp.dot(oa_ref[...], wbf_ref[D_GLA:, :], preferred_element_type=jnp.float32))

    def finish(y_ref):
        y = y_ref[...]
        h = x_ref[...] + y * _rms_scale(y) * gpost_ref[...]
        if fused:
            hn_ref[...] = (h * _rms_scale(h) * gpre_ref[...]).astype(hn_ref.dtype)
        h_ref[...] = h

    pl.when(i == 0)(lambda: matmul(y_refs[0]))
    for parity in range(2):
        @pl.when((i > 0) & (i < n_tiles) & (i % 2 == parity))
        def _():
            matmul(y_refs[parity])
            finish(y_refs[1 - parity])
    pl.when(i == n_tiles)(lambda: finish(y_refs[(n_tiles - 1) % 2]))


def _out_proj(layer, og, oa, w_out, x, g_post, g_pre=None):
    m = x.shape[0]
    fused = g_pre is not None
    n_tiles = m // OUT_TM
    mm_tile = lambda i: (jnp.minimum(i, n_tiles - 1), 0)
    fin_tile = lambda i: (jnp.maximum(i - 1, 0), 0)
    in_specs = [
        pl.BlockSpec((OUT_TM, D_GLA), mm_tile),
        pl.BlockSpec((OUT_TM, D_ATT), mm_tile),
        pl.BlockSpec((None, D_GLA + D_ATT, D_MODEL), lambda i: (layer, 0, 0), pipeline_mode=pl.Buffered(1)),
        pl.BlockSpec((OUT_TM, D_MODEL), fin_tile),
        pl.BlockSpec((None, 1, D_MODEL), lambda i: (layer, 0, 0)),
    ]
    out_specs = [pl.BlockSpec((OUT_TM, D_MODEL), fin_tile)]
    out_shape = [jax.ShapeDtypeStruct((m, D_MODEL), jnp.float32)]
    args = [og, oa, w_out, x, g_post]
    if fused:
        in_specs.append(pl.BlockSpec((None, 1, D_MODEL), lambda i: (layer + 1, 0, 0)))
        out_specs.append(pl.BlockSpec((OUT_TM, D_MODEL), fin_tile))
        out_shape.append(jax.ShapeDtypeStruct((m, D_MODEL), jnp.bfloat16))
        args.append(g_pre)
    return pl.pallas_call(
        functools.partial(_out_proj_kernel, fused, n_tiles),
        grid=(n_tiles + 1,),
        in_specs=in_specs,
        out_specs=out_specs,
        out_shape=out_shape,
        scratch_shapes=[pltpu.VMEM((D_GLA + D_ATT, D_MODEL), jnp.bfloat16),
                        pltpu.VMEM((OUT_TM, D_MODEL), jnp.float32),
                        pltpu.VMEM((OUT_TM, D_MODEL), jnp.float32)],
        compiler_params=pltpu.CompilerParams(
            dimension_semantics=("arbitrary",), vmem_limit_bytes=VMEM_LIMIT),
        name="out_proj",
    )(*args)


def kernel(x, w_in, w_out, g_pre, g_post, w_alpha, b_alpha, g_gla, g_att, rel_bias):
    batch, seq, d_model = x.shape
    depth = w_in.shape[0]
    h = x.reshape(batch * seq, d_model)
    w_in_t = jnp.swapaxes(w_in, 1, 2).reshape(depth * D_IN, d_model)
    walt = jnp.pad(jnp.swapaxes(w_alpha, 1, 2), ((0, 0), (0, 0), (0, LANES - GATE_RANK))).astype(jnp.bfloat16)
    bcol = b_alpha[:, :, None]
    bias_rows, band_mask = _band_bias_rows(rel_bias), _band_mask()
    row = lambda p: p[:, None, :]
    hn = _norm_call(0, h, row(g_pre))
    for l in range(depth):
        kt, gat = _proj_nt(l, hn, w_in_t)
        (zg,) = _proj_nn(l, hn, w_in_t, _GLA_ROWS)
        za, og = _proj_nn(l, hn, w_in_t, _ATT_ROWS, gla=(zg, kt, gat, walt, bcol, row(g_gla), batch, seq))
        oa = _band_attn(l, za, kt, bias_rows, band_mask, row(g_att), batch, seq)
        if l + 1 < depth:
            h, hn = _out_proj(l, og, oa, w_out, h, row(g_post), row(g_pre))
        else:
            (h,) = _out_proj(l, og, oa, w_out, h, row(g_post))
    return h.reshape(batch, seq, d_model)
```

```python
import functools
import math

import jax
import jax.numpy as jnp
import numpy as np
from jax import lax
from jax.experimental import pallas as pl
from jax.experimental.pallas import tpu as pltpu

D_MODEL = 2048
CHUNK = 64
D_GLA = 1024
D_ATT = 1024
GLA_HEADS = 4
GLA_DK = 128
GLA_DV = 256
GLA_KW = GLA_HEADS * GLA_DK
GATE_RANK = 16
GLA_TAU = 16.0
ATT_HEADS = 8
ATT_HD = 128
LEFT_CHUNKS = 8
REL_CLIP = 128
EPS = 1e-6
LOG2E = math.log2(math.e)

LANES = 128

W_GA = 2 * GLA_KW + 2 * D_GLA
W_AQ = W_GA + GATE_RANK
D_IN = W_AQ + 4 * D_ATT
ROW_ALIGN = GATE_RANK
PROJ_TN = 512
Z_GQ, Z_GV, Z_GG = 0, GLA_KW, GLA_KW + D_GLA
Z_AQ, Z_AV, Z_AG = 0, D_ATT, 2 * D_ATT
KT_G, KT_A = 0, GLA_KW
D_KT = GLA_KW + D_ATT

PROJ_TM = 2048
MIX_PIECE_ROWS, MIX_PIECE_COLS = 256, 256
NORM_TM = 1024
OUT_TM = 256
GLA_GROUP = 256
ATT_TQ = 128
ATT_HEADS_PER_STEP = 4
ATT_WIN = ATT_TQ + LEFT_CHUNKS * CHUNK
CAST_ROWS = 256
VMEM_LIMIT = 48 * 1024 * 1024
VMEM_LIMIT_FUSED = 56 * 1024 * 1024


def _silu(x):
    half = 0.5 * x
    return half + half * jnp.tanh(half)


def _log_sigmoid(x):
    return jnp.minimum(x, 0.0) - jnp.log(1.0 + jnp.exp(-jnp.abs(x)))


def _rms_scale(x):
    return lax.rsqrt(jnp.mean(x * x, axis=-1, keepdims=True) + EPS)


def _norm_kernel(x_ref, gpre_ref, hn_ref):
    x = x_ref[...]
    hn_ref[...] = (x * _rms_scale(x) * gpre_ref[...]).astype(hn_ref.dtype)


def _norm_call(layer, x, g_pre):
    rows = x.shape[0]
    return pl.pallas_call(
        _norm_kernel,
        grid=(rows // NORM_TM,),
        in_specs=[pl.BlockSpec((NORM_TM, D_MODEL), lambda i: (i, 0)),
                  pl.BlockSpec((None, 1, D_MODEL), lambda i: (layer, 0, 0))],
        out_specs=pl.BlockSpec((NORM_TM, D_MODEL), lambda i: (i, 0)),
        out_shape=jax.ShapeDtypeStruct((rows, D_MODEL), jnp.bfloat16),
        compiler_params=pltpu.CompilerParams(
            dimension_semantics=("parallel",), vmem_limit_bytes=VMEM_LIMIT),
        name="norm",
    )(x, g_pre)


def _split3_bf16(x):
    hi = x.astype(jnp.bfloat16)
    r1 = x - hi.astype(jnp.float32)
    mid = r1.astype(jnp.bfloat16)
    lo = (r1 - mid.astype(jnp.float32)).astype(jnp.bfloat16)
    return hi, mid, lo


def _gla_unit(side_work, q_ref, kt_ref, v_ref, gate_ref, gat_ref, walt_ref, bcol_ref, g_ref, o_ref,
              st_ref, parts_ref, sums_ref, u_ref, a_ref):
    seq = q_ref.shape[0]
    n_groups = seq // GLA_GROUP
    chunks_per_group = GLA_GROUP // CHUNK
    r = lax.broadcasted_iota(jnp.int32, (3 * GLA_GROUP, 2 * GLA_GROUP), 0) % GLA_GROUP
    c = lax.broadcasted_iota(jnp.int32, (3 * GLA_GROUP, 2 * GLA_GROUP), 1)
    sums = jnp.where((r // CHUNK == (c % GLA_GROUP) // CHUNK) & (r <= c), 1.0, 0.0).astype(jnp.bfloat16)
    frame_chunk = lax.broadcasted_iota(jnp.int32, (GLA_DK, GLA_GROUP), 1) // CHUNK
    ga_pad = jnp.zeros((LANES - GATE_RANK, GLA_GROUP), jnp.bfloat16)
    g = g_ref[...]

    def gate_step(gi):
        ga = jnp.concatenate([gat_ref[:, pl.ds(gi * GLA_GROUP, GLA_GROUP)], ga_pad], axis=0)
        pre = jnp.dot(walt_ref[...], ga, preferred_element_type=jnp.float32) + bcol_ref[...]
        log_a = _log_sigmoid(pre) * (1.0 / GLA_TAU)
        parts_ref[pl.ds(gi * GLA_DK, GLA_DK), :] = jnp.concatenate(_split3_bf16(log_a), axis=1)

    def sums_step():
        sums_ref[...] = jnp.dot(parts_ref[...], sums, preferred_element_type=jnp.float32)

    def decay_step(gi):
        cols = pl.ds(gi * GLA_GROUP, GLA_GROUP)
        both = sums_ref[pl.ds(gi * GLA_DK, GLA_DK), :]
        run, total = both[:, :GLA_GROUP], both[:, GLA_GROUP:]
        k_dec = (kt_ref[:, cols].astype(jnp.float32) * jnp.exp(total - run)).astype(jnp.bfloat16)
        k_own = jnp.concatenate([jnp.where(frame_chunk == ci, k_dec, jnp.zeros_like(k_dec))
                                 for ci in range(chunks_per_group)], axis=0)
        u = jnp.dot(k_own, v_ref[cols, :], preferred_element_type=jnp.float32)
        a = jnp.exp(total)
        for ci in range(chunks_per_group):
            u_ref[gi, ci] = u[ci * GLA_DK:(ci + 1) * GLA_DK]
            a_ref[gi, ci] = jnp.broadcast_to(a[:, ci * CHUNK:ci * CHUNK + 1], (GLA_DK, GLA_DV))

    def state_step(gi):
        s_c = st_ref[...]
        for ci in range(chunks_per_group):
            s_c = a_ref[gi, ci] * s_c + u_ref[gi, ci]
            rows = pl.ds(gi * GLA_GROUP + ci * CHUNK, CHUNK)
            o = jnp.dot(q_ref[rows, :], s_c.astype(jnp.bfloat16), preferred_element_type=jnp.float32)
            o = o * _rms_scale(o) * g
            o_ref[rows, :] = (o * _silu(gate_ref[rows, :].astype(jnp.float32))).astype(o_ref.dtype)
        st_ref[...] = s_c

    stages = ([functools.partial(gate_step, gi) for gi in range(n_groups)] + [sums_step]
              + [functools.partial(decay_step, gi) for gi in range(n_groups)]
              + [functools.partial(state_step, gi) for gi in range(n_groups)])
    side_at = {(k * len(stages)) // len(side_work): fn for k, fn in enumerate(side_work)}
    st_ref[...] = jnp.zeros_like(st_ref)
    for k, stage in enumerate(stages):
        if k in side_at:
            side_at[k]()
        stage()


def _gla_specs(layer, seq, unit):
    b_h = lambda off: (lambda *idx: (unit(*idx)[0], off + unit(*idx)[1]))
    in_specs = [
        pl.BlockSpec((seq, GLA_DK), b_h(Z_GQ // GLA_DK)),
        pl.BlockSpec((GLA_DK, seq), lambda *idx: (KT_G // GLA_DK + unit(*idx)[1], unit(*idx)[0])),
        pl.BlockSpec((seq, GLA_DV), b_h(Z_GV // GLA_DV)),
        pl.BlockSpec((seq, GLA_DV), b_h(Z_GG // GLA_DV)),
        pl.BlockSpec((None, GATE_RANK, seq), lambda *idx: (0, 0, unit(*idx)[0])),
        pl.BlockSpec((None, GLA_DK, LANES), lambda *idx: (layer, unit(*idx)[1], 0)),
        pl.BlockSpec((None, GLA_DK, 1), lambda *idx: (layer, unit(*idx)[1], 0)),
        pl.BlockSpec((None, 1, GLA_DV), lambda *idx: (layer, 0, unit(*idx)[1])),
    ]
    out_spec = pl.BlockSpec((seq, GLA_DV), lambda *idx: unit(*idx))
    return in_specs, out_spec


def _gla_scratch(seq):
    n_groups = seq // GLA_GROUP
    chunks_per_group = GLA_GROUP // CHUNK
    return [pltpu.VMEM((GLA_DK, GLA_DV), jnp.float32),
            pltpu.VMEM((n_groups * GLA_DK, 3 * GLA_GROUP), jnp.bfloat16),
            pltpu.VMEM((n_groups * GLA_DK, 2 * GLA_GROUP), jnp.float32),
            pltpu.VMEM((n_groups, chunks_per_group, GLA_DK, GLA_DV), jnp.float32),
            pltpu.VMEM((n_groups, chunks_per_group, GLA_DK, GLA_DV), jnp.float32)]


def _attn_scores(q_ref, kt_ref, bias_ref, q_start, k_start, width):
    s = jnp.dot(q_ref[q_start:q_start + ATT_TQ, :], kt_ref[:, k_start:k_start + width],
                preferred_element_type=jnp.float32)
    return s + bias_ref[:, ATT_WIN - width:]


def _attn_softmax(s2):
    m = jnp.max(s2, axis=-1, keepdims=True)
    e = jnp.exp2(s2 - m)
    return e.astype(jnp.bfloat16), jnp.sum(e, axis=-1, keepdims=True)


def _attn_finish(p, denom, v_ref, gate_ref, g, o_ref, q_start, k_start, width):
    o = jnp.dot(p, v_ref[k_start:k_start + width, :], preferred_element_type=jnp.float32)
    o = o * (1.0 / denom)
    o = o * _rms_scale(o) * g
    rows = slice(q_start, q_start + ATT_TQ)
    o_ref[rows, :] = (o * _silu(gate_ref[rows, :].astype(jnp.float32))).astype(o_ref.dtype)


def _attn_kernel(q_ref, kt_ref, v_ref, gate_ref, brow_ref, mask_ref, g_ref, o_ref, bias_ref):
    seq = q_ref.shape[0]
    n_tiles = seq // ATT_TQ
    heads = range(ATT_HEADS_PER_STEP)
    cols = [pl.ds(j * ATT_HD, ATT_HD) for j in heads]
    g = [g_ref[:, cols[j]] for j in heads]

    @pl.when(pl.program_id(1) == 0)
    def _():
        for j in heads:
            rows = jnp.broadcast_to(brow_ref[j], (ATT_TQ, ATT_TQ + ATT_WIN))
            bias_ref[j] = pltpu.roll(rows, 0, 1, stride=1, stride_axis=0)[:, :ATT_WIN] + mask_ref[...]

    def window(t):
        k_start = max(0, (t + 1) * ATT_TQ - ATT_WIN)
        return t * ATT_TQ, k_start, (t + 1) * ATT_TQ - k_start

    scores, probs = {}, {}
    for step in range(n_tiles + 2):
        for j in heads:
            if step < n_tiles:
                scores[step, j] = _attn_scores(q_ref.at[:, cols[j]], kt_ref.at[cols[j], :], bias_ref.at[j], *window(step))
            if 0 <= step - 1 < n_tiles:
                probs[step - 1, j] = _attn_softmax(scores.pop((step - 1, j)))
            if 0 <= step - 2:
                _attn_finish(*probs.pop((step - 2, j)), v_ref.at[:, cols[j]], gate_ref.at[:, cols[j]], g[j],
                             o_ref.at[:, cols[j]], *window(step - 2))


def _band_attn(layer, z, kt, bias_rows, band_mask, g_att, batch, seq):
    width = ATT_HEADS_PER_STEP * ATT_HD
    blk = lambda off: off // width
    b_h = lambda off: (lambda h, b: (b, off + h))
    return pl.pallas_call(
        _attn_kernel,
        grid=(ATT_HEADS // ATT_HEADS_PER_STEP, batch),
        in_specs=[
            pl.BlockSpec((seq, width), b_h(blk(Z_AQ))),
            pl.BlockSpec((width, seq), lambda h, b: (blk(KT_A) + h, b)),
            pl.BlockSpec((seq, width), b_h(blk(Z_AV))),
            pl.BlockSpec((seq, width), b_h(blk(Z_AG))),
            pl.BlockSpec((None, ATT_HEADS_PER_STEP, 1, ATT_TQ + ATT_WIN), lambda h, b: (layer, h, 0, 0)),
            pl.BlockSpec((ATT_TQ, ATT_WIN), lambda h, b: (0, 0)),
            pl.BlockSpec((None, 1, width), lambda h, b: (layer, 0, h)),
        ],
        out_specs=pl.BlockSpec((seq, width), lambda h, b: (b, h)),
        out_shape=jax.ShapeDtypeStruct((z.shape[0], D_ATT), jnp.bfloat16),
        scratch_shapes=[pltpu.VMEM((ATT_HEADS_PER_STEP, ATT_TQ, ATT_WIN), jnp.float32)],
        compiler_params=pltpu.CompilerParams(
            dimension_semantics=("arbitrary", "arbitrary"), vmem_limit_bytes=VMEM_LIMIT),
        name="band_attn",
    )(z, kt, z, z, bias_rows, band_mask, g_att)


def _band_bias_rows(rel_bias):
    lead = rel_bias.shape[:-1]
    rb = rel_bias.astype(jnp.float32) * LOG2E
    far = rb[..., 2 * REL_CLIP:]
    row = jnp.concatenate([jnp.broadcast_to(far, lead + (ATT_WIN - REL_CLIP - (ATT_TQ - 1),)),
                           rb[..., 2 * REL_CLIP - 1:0:-1],
                           jnp.zeros(lead + (1,), jnp.float32),
                           jnp.broadcast_to(far, lead + (ATT_TQ - 1,))], axis=-1)
    return row[..., None, :]


def _band_mask():
    i = np.arange(ATT_TQ)[:, None]
    j = np.arange(ATT_WIN)[None, :]
    first = (i // CHUNK) * CHUNK
    in_band = (j >= first) & (j < first + (LEFT_CHUNKS + 1) * CHUNK)
    return jnp.asarray(np.where(in_band, 0.0, -1e30), jnp.float32)


_GLA_ROWS = [0] + [2 * GLA_KW + t * PROJ_TN for t in range(2 * D_GLA // PROJ_TN)]
_ATT_ROWS = ([W_AQ + t * PROJ_TN for t in range(D_ATT // PROJ_TN)]
             + [W_AQ + 2 * D_ATT + t * PROJ_TN for t in range(2 * D_ATT // PROJ_TN)])
_NT_ROWS = [GLA_KW] + [W_AQ + D_ATT + t * PROJ_TN for t in range(D_ATT // PROJ_TN)]


def _table_lookup(table, j):
    val = table[0] + j * 0
    for t in range(1, len(table)):
        val = jnp.where(j >= t, table[t], val)
    return val


def _cast_rows(src_ref, dst_ref, n_rows):
    def body(r, carry):
        rows = pl.ds(pl.multiple_of(r * CAST_ROWS, CAST_ROWS), CAST_ROWS)
        dst_ref[rows, :] = src_ref[rows, :].astype(dst_ref.dtype)
        return carry
    lax.fori_loop(0, n_rows // CAST_ROWS, body, 0)


def _w_rows_spec(layer, table, n_rows, tiles):
    units = [(layer * D_IN + r) // ROW_ALIGN for r in table]
    return pl.BlockSpec((pl.Element(n_rows), pl.Element(D_MODEL)),
                        lambda s: (_table_lookup(units, s // tiles) * ROW_ALIGN, 0))


def _proj_nn_kernel(tiles, n_units, hn_ref, wt_ref, *refs):
    if n_units:
        gla_in, (z_ref, og_ref, wbf_ref), gla_scratch = refs[:8], refs[8:11], refs[11:]
    else:
        z_ref, wbf_ref = refs
    s = pl.program_id(0)

    @pl.when(s % tiles == 0)
    def _():
        _cast_rows(wt_ref, wbf_ref, PROJ_TN)

    def matmul_piece(r0, n_rows, c0, n_cols):
        def run():
            rows, cols = slice(r0, r0 + n_rows), slice(c0, c0 + n_cols)
            z_ref[rows, cols] = lax.dot_general(hn_ref[rows, :], wbf_ref[cols, :], (((1,), (1,)), ((), ())),
                                                preferred_element_type=jnp.float32).astype(z_ref.dtype)
        return run

    matmul = matmul_piece(0, PROJ_TM, 0, PROJ_TN)

    if n_units:
        @pl.when(s < n_units)
        def _():
            pieces = [matmul_piece(r0, MIX_PIECE_ROWS, c0, MIX_PIECE_COLS)
                      for r0 in range(0, PROJ_TM, MIX_PIECE_ROWS) for c0 in range(0, PROJ_TN, MIX_PIECE_COLS)]
            _gla_unit(pieces, *gla_in, og_ref, *gla_scratch)

        pl.when(s >= n_units)(matmul)
    else:
        matmul()


def _proj_nn(layer, hn, w_in_t, table, gla=None):
    m = hn.shape[0]
    tiles = m // PROJ_TM
    steps = len(table) * tiles
    in_specs = [pl.BlockSpec((PROJ_TM, D_MODEL), lambda s: (s % tiles, 0)),
                _w_rows_spec(layer, table, PROJ_TN, tiles)]
    out_specs = [pl.BlockSpec((PROJ_TM, PROJ_TN), lambda s: (s % tiles, s // tiles))]
    out_shape = [jax.ShapeDtypeStruct((m, len(table) * PROJ_TN), jnp.bfloat16)]
    scratch = [pltpu.VMEM((PROJ_TN, D_MODEL), jnp.bfloat16)]
    args = [hn, w_in_t]
    n_units = 0
    if gla is not None:
        zg, kt, gat, walt, bcol, g_gla, batch, seq = gla
        n_units = batch * GLA_HEADS
        assert n_units <= steps

        def unit(s):
            u = jnp.minimum(s, n_units - 1)
            return u // GLA_HEADS, u % GLA_HEADS

        g_in, g_out = _gla_specs(layer, seq, unit)
        in_specs += g_in
        out_specs.append(g_out)
        out_shape.append(jax.ShapeDtypeStruct((m, D_GLA), jnp.bfloat16))
        scratch += _gla_scratch(seq)
        args += [zg, kt, zg, zg, gat, walt, bcol, g_gla]
    return pl.pallas_call(
        functools.partial(_proj_nn_kernel, tiles, n_units),
        grid=(steps,),
        in_specs=in_specs,
        out_specs=out_specs,
        out_shape=out_shape,
        scratch_shapes=scratch,
        compiler_params=pltpu.CompilerParams(
            dimension_semantics=("arbitrary",),
            vmem_limit_bytes=VMEM_LIMIT_FUSED if n_units else VMEM_LIMIT),
        name="proj_att_gla" if n_units else "proj_gla",
    )(*args)


def _proj_nt_kernel(tiles, hn_ref, wt_ref, wga_ref, kt_ref, gat_ref, wbf_ref):
    s = pl.program_id(0)

    @pl.when(s % tiles == 0)
    def _():
        _cast_rows(wt_ref, wbf_ref, PROJ_TN)

    @pl.when(s == 0)
    def _():
        wbf_ref[PROJ_TN:, :] = wga_ref[...].astype(wbf_ref.dtype)

    res = lax.dot_general(wbf_ref[...], hn_ref[...], (((1,), (1,)), ((), ())),
                          preferred_element_type=jnp.float32)
    scale = jnp.where(s < tiles, GLA_DK ** -0.5, ATT_HD ** -0.5 * LOG2E)
    kt_ref[...] = (res[:PROJ_TN] * scale).astype(kt_ref.dtype)
    gat_ref[...] = res[PROJ_TN:].astype(gat_ref.dtype)


def _proj_nt(layer, hn, w_in_t):
    m = hn.shape[0]
    tiles = m // PROJ_TM
    return pl.pallas_call(
        functools.partial(_proj_nt_kernel, tiles),
        grid=(len(_NT_ROWS) * tiles,),
        in_specs=[pl.BlockSpec((PROJ_TM, D_MODEL), lambda s: (s % tiles, 0)),
                  _w_rows_spec(layer, _NT_ROWS, PROJ_TN, tiles),
                  _w_rows_spec(layer, [W_GA], GATE_RANK, len(_NT_ROWS) * tiles)],
        out_specs=[pl.BlockSpec((PROJ_TN, PROJ_TM), lambda s: (s // tiles, s % tiles)),
                   pl.BlockSpec((None, GATE_RANK, PROJ_TM), lambda s: (s // tiles, 0, s % tiles))],
        out_shape=[jax.ShapeDtypeStruct((D_KT, m), jnp.bfloat16),
                   jax.ShapeDtypeStruct((len(_NT_ROWS), GATE_RANK, m), jnp.bfloat16)],
        scratch_shapes=[pltpu.VMEM((PROJ_TN + GATE_RANK, D_MODEL), jnp.bfloat16)],
        compiler_params=pltpu.CompilerParams(
            dimension_semantics=("arbitrary",), vmem_limit_bytes=VMEM_LIMIT),
        name="proj_nt",
    )(hn, w_in_t, w_in_t)


def _out_proj_kernel(fused, n_tiles, og_ref, oa_ref, w_ref, x_ref, gpost_ref, *rest):
    if fused:
        gpre_ref, h_ref, hn_ref, wbf_ref, *y_refs = rest
    else:
        h_ref, wbf_ref, *y_refs = rest
    i = pl.program_id(0)

    @pl.when(i == 0)
    def _():
        _cast_rows(w_ref, wbf_ref, D_GLA + D_ATT)

    def matmul(y_ref):
        y_ref[...] = (jnp.dot(og_ref[...], wbf_ref[:D_GLA, :], preferred_element_type=jnp.float32)
                      + jnp.dot(oa_ref[...], wbf_ref[D_GLA:, :], preferred_element_type=jnp.float32))

    def finish(y_ref):
        y = y_ref[...]
        h = x_ref[...] + y * _rms_scale(y) * gpost_ref[...]
        if fused:
            hn_ref[...] = (h * _rms_scale(h) * gpre_ref[...]).astype(hn_ref.dtype)
        h_ref[...] = h

    pl.when(i == 0)(lambda: matmul(y_refs[0]))
    for parity in range(2):
        @pl.when((i > 0) & (i < n_tiles) & (i % 2 == parity))
        def _():
            matmul(y_refs[parity])
            finish(y_refs[1 - parity])
    pl.when(i == n_tiles)(lambda: finish(y_refs[(n_tiles - 1) % 2]))


def _out_proj(layer, og, oa, w_out, x, g_post, g_pre=None):
    m = x.shape[0]
    fused = g_pre is not None
    n_tiles = m // OUT_TM
    mm_tile = lambda i: (jnp.minimum(i, n_tiles - 1), 0)
    fin_tile = lambda i: (jnp.maximum(i - 1, 0), 0)
    in_specs = [
        pl.BlockSpec((OUT_TM, D_GLA), mm_tile),
        pl.BlockSpec((OUT_TM, D_ATT), mm_tile),
        pl.BlockSpec((None, D_GLA + D_ATT, D_MODEL), lambda i: (layer, 0, 0), pipeline_mode=pl.Buffered(1)),
        pl.BlockSpec((OUT_TM, D_MODEL), fin_tile),
        pl.BlockSpec((None, 1, D_MODEL), lambda i: (layer, 0, 0)),
    ]
    out_specs = [pl.BlockSpec((OUT_TM, D_MODEL), fin_tile)]
    out_shape = [jax.ShapeDtypeStruct((m, D_MODEL), jnp.float32)]
    args = [og, oa, w_out, x, g_post]
    if fused:
        in_specs.append(pl.BlockSpec((None, 1, D_MODEL), lambda i: (layer + 1, 0, 0)))
        out_specs.append(pl.BlockSpec((OUT_TM, D_MODEL), fin_tile))
        out_shape.append(jax.ShapeDtypeStruct((m, D_MODEL), jnp.bfloat16))
        args.append(g_pre)
    return pl.pallas_call(
        functools.partial(_out_proj_kernel, fused, n_tiles),
        grid=(n_tiles + 1,),
        in_specs=in_specs,
        out_specs=out_specs,
        out_shape=out_shape,
        scratch_shapes=[pltpu.VMEM((D_GLA + D_ATT, D_MODEL), jnp.bfloat16),
                        pltpu.VMEM((OUT_TM, D_MODEL), jnp.float32),
                        pltpu.VMEM((OUT_TM, D_MODEL), jnp.float32)],
        compiler_params=pltpu.CompilerParams(
            dimension_semantics=("arbitrary",), vmem_limit_bytes=VMEM_LIMIT),
        name="out_proj",
    )(*args)


def kernel(x, w_in, w_out, g_pre, g_post, w_alpha, b_alpha, g_gla, g_att, rel_bias):
    batch, seq, d_model = x.shape
    depth = w_in.shape[0]
    h = x.reshape(batch * seq, d_model)
    w_in_t = jnp.swapaxes(w_in, 1, 2).reshape(depth * D_IN, d_model)
    walt = jnp.pad(jnp.swapaxes(w_alpha, 1, 2), ((0, 0), (0, 0), (0, LANES - GATE_RANK))).astype(jnp.bfloat16)
    bcol = b_alpha[:, :, None]
    bias_rows, band_mask = _band_bias_rows(rel_bias), _band_mask()
    row = lambda p: p[:, None, :]
    hn = _norm_call(0, h, row(g_pre))
    for l in range(depth):
        kt, gat = _proj_nt(l, hn, w_in_t)
        (zg,) = _proj_nn(l, hn, w_in_t, _GLA_ROWS)
        za, og = _proj_nn(l, hn, w_in_t, _ATT_ROWS, gla=(zg, kt, gat, walt, bcol, row(g_gla), batch, seq))
        oa = _band_attn(l, za, kt, bias_rows, band_mask, row(g_att), batch, seq)
        if l + 1 < depth:
            h, hn = _out_proj(l, og, oa, w_out, h, row(g_post), row(g_pre))
        else:
            (h,) = _out_proj(l, og, oa, w_out, h, row(g_post))
    return h.reshape(batch, seq, d_model)
```

```python
import functools
import math

import jax
import jax.numpy as jnp
import numpy as np
from jax import lax
from jax.experimental import pallas as pl
from jax.experimental.pallas import tpu as pltpu

D_MODEL = 2048
CHUNK = 64
D_GLA = 1024
D_ATT = 1024
GLA_HEADS = 4
GLA_DK = 128
GLA_DV = 256
GLA_KW = GLA_HEADS * GLA_DK
GATE_RANK = 16
GLA_TAU = 16.0
ATT_HEADS = 8
ATT_HD = 128
LEFT_CHUNKS = 8
REL_CLIP = 128
EPS = 1e-6
LOG2E = math.log2(math.e)

LANES = 128

W_GA = 2 * GLA_KW + 2 * D_GLA
W_AQ = W_GA + GATE_RANK
D_IN = W_AQ + 4 * D_ATT
ROW_ALIGN = GATE_RANK
PROJ_TN = 512
Z_GQ, Z_GV, Z_GG = 0, GLA_KW, GLA_KW + D_GLA
Z_AQ, Z_AV, Z_AG = 0, D_ATT, 2 * D_ATT
KT_G, KT_A = 0, GLA_KW
D_KT = GLA_KW + D_ATT

PROJ_TM = 2048
MIX_PIECE_ROWS, MIX_PIECE_COLS = 256, 256
NORM_TM = 1024
OUT_TM = 256
GLA_GROUP = 256
ATT_TQ = 128
ATT_HEADS_PER_STEP = 2
ATT_WIN = ATT_TQ + LEFT_CHUNKS * CHUNK
CAST_ROWS = 256
VMEM_LIMIT = 48 * 1024 * 1024
VMEM_LIMIT_FUSED = 56 * 1024 * 1024


def _silu(x):
    half = 0.5 * x
    return half + half * jnp.tanh(half)


def _log_sigmoid(x):
    return jnp.minimum(x, 0.0) - jnp.log(1.0 + jnp.exp(-jnp.abs(x)))


def _rms_scale(x):
    return lax.rsqrt(jnp.mean(x * x, axis=-1, keepdims=True) + EPS)


def _norm_kernel(x_ref, gpre_ref, hn_ref):
    x = x_ref[...]
    hn_ref[...] = (x * _rms_scale(x) * gpre_ref[...]).astype(hn_ref.dtype)


def _norm_call(layer, x, g_pre):
    rows = x.shape[0]
    return pl.pallas_call(
        _norm_kernel,
        grid=(rows // NORM_TM,),
        in_specs=[pl.BlockSpec((NORM_TM, D_MODEL), lambda i: (i, 0)),
                  pl.BlockSpec((None, 1, D_MODEL), lambda i: (layer, 0, 0))],
        out_specs=pl.BlockSpec((NORM_TM, D_MODEL), lambda i: (i, 0)),
        out_shape=jax.ShapeDtypeStruct((rows, D_MODEL), jnp.bfloat16),
        compiler_params=pltpu.CompilerParams(
            dimension_semantics=("parallel",), vmem_limit_bytes=VMEM_LIMIT),
        name="norm",
    )(x, g_pre)


def _split3_bf16(x):
    hi = x.astype(jnp.bfloat16)
    r1 = x - hi.astype(jnp.float32)
    mid = r1.astype(jnp.bfloat16)
    lo = (r1 - mid.astype(jnp.float32)).astype(jnp.bfloat16)
    return hi, mid, lo


def _gla_unit(side_work, q_ref, kt_ref, v_ref, gate_ref, gat_ref, walt_ref, bcol_ref, g_ref, o_ref,
              st_ref, parts_ref, sums_ref, u_ref, a_ref):
    seq = q_ref.shape[0]
    n_groups = seq // GLA_GROUP
    chunks_per_group = GLA_GROUP // CHUNK
    r = lax.broadcasted_iota(jnp.int32, (3 * GLA_GROUP, 2 * GLA_GROUP), 0) % GLA_GROUP
    c = lax.broadcasted_iota(jnp.int32, (3 * GLA_GROUP, 2 * GLA_GROUP), 1)
    sums = jnp.where((r // CHUNK == (c % GLA_GROUP) // CHUNK) & (r <= c), 1.0, 0.0).astype(jnp.bfloat16)
    frame_chunk = lax.broadcasted_iota(jnp.int32, (GLA_DK, GLA_GROUP), 1) // CHUNK
    ga_pad = jnp.zeros((LANES - GATE_RANK, GLA_GROUP), jnp.bfloat16)
    g = g_ref[...]

    def gate_step(gi):
        ga = jnp.concatenate([gat_ref[:, pl.ds(gi * GLA_GROUP, GLA_GROUP)], ga_pad], axis=0)
        pre = jnp.dot(walt_ref[...], ga, preferred_element_type=jnp.float32) + bcol_ref[...]
        log_a = _log_sigmoid(pre) * (1.0 / GLA_TAU)
        parts_ref[pl.ds(gi * GLA_DK, GLA_DK), :] = jnp.concatenate(_split3_bf16(log_a), axis=1)

    def sums_step():
        sums_ref[...] = jnp.dot(parts_ref[...], sums, preferred_element_type=jnp.float32)

    def decay_step(gi):
        cols = pl.ds(gi * GLA_GROUP, GLA_GROUP)
        both = sums_ref[pl.ds(gi * GLA_DK, GLA_DK), :]
        run, total = both[:, :GLA_GROUP], both[:, GLA_GROUP:]
        k_dec = (kt_ref[:, cols].astype(jnp.float32) * jnp.exp(total - run)).astype(jnp.bfloat16)
        k_own = jnp.concatenate([jnp.where(frame_chunk == ci, k_dec, jnp.zeros_like(k_dec))
                                 for ci in range(chunks_per_group)], axis=0)
        u = jnp.dot(k_own, v_ref[cols, :], preferred_element_type=jnp.float32)
        a = jnp.exp(total)
        for ci in range(chunks_per_group):
            u_ref[gi, ci] = u[ci * GLA_DK:(ci + 1) * GLA_DK]
            a_ref[gi, ci] = jnp.broadcast_to(a[:, ci * CHUNK:ci * CHUNK + 1], (GLA_DK, GLA_DV))

    def state_step(gi):
        s_c = st_ref[...]
        for ci in range(chunks_per_group):
            s_c = a_ref[gi, ci] * s_c + u_ref[gi, ci]
            rows = pl.ds(gi * GLA_GROUP + ci * CHUNK, CHUNK)
            o = jnp.dot(q_ref[rows, :], s_c.astype(jnp.bfloat16), preferred_element_type=jnp.float32)
            o = o * _rms_scale(o) * g
            o_ref[rows, :] = (o * _silu(gate_ref[rows, :].astype(jnp.float32))).astype(o_ref.dtype)
        st_ref[...] = s_c

    stages = ([functools.partial(gate_step, gi) for gi in range(n_groups)] + [sums_step]
              + [functools.partial(decay_step, gi) for gi in range(n_groups)]
              + [functools.partial(state_step, gi) for gi in range(n_groups)])
    side_at = {(k * len(stages)) // len(side_work): fn for k, fn in enumerate(side_work)}
    st_ref[...] = jnp.zeros_like(st_ref)
    for k, stage in enumerate(stages):
        if k in side_at:
            side_at[k]()
        stage()


def _gla_specs(layer, seq, unit):
    b_h = lambda off: (lambda *idx: (unit(*idx)[0], off + unit(*idx)[1]))
    in_specs = [
        pl.BlockSpec((seq, GLA_DK), b_h(Z_GQ // GLA_DK)),
        pl.BlockSpec((GLA_DK, seq), lambda *idx: (KT_G // GLA_DK + unit(*idx)[1], unit(*idx)[0])),
        pl.BlockSpec((seq, GLA_DV), b_h(Z_GV // GLA_DV)),
        pl.BlockSpec((seq, GLA_DV), b_h(Z_GG // GLA_DV)),
        pl.BlockSpec((None, GATE_RANK, seq), lambda *idx: (0, 0, unit(*idx)[0])),
        pl.BlockSpec((None, GLA_DK, LANES), lambda *idx: (layer, unit(*idx)[1], 0)),
        pl.BlockSpec((None, GLA_DK, 1), lambda *idx: (layer, unit(*idx)[1], 0)),
        pl.BlockSpec((None, 1, GLA_DV), lambda *idx: (layer, 0, unit(*idx)[1])),
    ]
    out_spec = pl.BlockSpec((seq, GLA_DV), lambda *idx: unit(*idx))
    return in_specs, out_spec


def _gla_scratch(seq):
    n_groups = seq // GLA_GROUP
    chunks_per_group = GLA_GROUP // CHUNK
    return [pltpu.VMEM((GLA_DK, GLA_DV), jnp.float32),
            pltpu.VMEM((n_groups * GLA_DK, 3 * GLA_GROUP), jnp.bfloat16),
            pltpu.VMEM((n_groups * GLA_DK, 2 * GLA_GROUP), jnp.float32),
            pltpu.VMEM((n_groups, chunks_per_group, GLA_DK, GLA_DV), jnp.float32),
            pltpu.VMEM((n_groups, chunks_per_group, GLA_DK, GLA_DV), jnp.float32)]


def _attn_scores(q_ref, kt_ref, bias_ref, q_start, k_start, width):
    s = jnp.dot(q_ref[q_start:q_start + ATT_TQ, :], kt_ref[:, k_start:k_start + width],
                preferred_element_type=jnp.float32)
    return s + bias_ref[:, ATT_WIN - width:]


def _attn_softmax(s2):
    m = jnp.max(s2, axis=-1, keepdims=True)
    e = jnp.exp2(s2 - m)
    return e.astype(jnp.bfloat16), jnp.sum(e, axis=-1, keepdims=True)


def _attn_finish(p, denom, v_ref, gate_ref, g, o_ref, q_start, k_start, width):
    o = jnp.dot(p, v_ref[k_start:k_start + width, :], preferred_element_type=jnp.float32)
    o = o * (1.0 / denom)
    o = o * _rms_scale(o) * g
    rows = slice(q_start, q_start + ATT_TQ)
    o_ref[rows, :] = (o * _silu(gate_ref[rows, :].astype(jnp.float32))).astype(o_ref.dtype)


def _attn_kernel(q_ref, kt_ref, v_ref, gate_ref, brow_ref, mask_ref, g_ref, o_ref, bias_ref):
    seq = q_ref.shape[0]
    n_tiles = seq // ATT_TQ
    heads = range(ATT_HEADS_PER_STEP)
    cols = [pl.ds(j * ATT_HD, ATT_HD) for j in heads]
    g = [g_ref[:, cols[j]] for j in heads]

    @pl.when(pl.program_id(1) == 0)
    def _():
        for j in heads:
            rows = jnp.broadcast_to(brow_ref[j], (ATT_TQ, ATT_TQ + ATT_WIN))
            bias_ref[j] = pltpu.roll(rows, 0, 1, stride=1, stride_axis=0)[:, :ATT_WIN] + mask_ref[...]

    def window(t):
        k_start = max(0, (t + 1) * ATT_TQ - ATT_WIN)
        return t * ATT_TQ, k_start, (t + 1) * ATT_TQ - k_start

    scores, probs = {}, {}
    for step in range(n_tiles + 2):
        for j in heads:
            if step < n_tiles:
                scores[step, j] = _attn_scores(q_ref.at[:, cols[j]], kt_ref.at[cols[j], :], bias_ref.at[j], *window(step))
            if 0 <= step - 1 < n_tiles:
                probs[step - 1, j] = _attn_softmax(scores.pop((step - 1, j)))
            if 0 <= step - 2:
                _attn_finish(*probs.pop((step - 2, j)), v_ref.at[:, cols[j]], gate_ref.at[:, cols[j]], g[j],
                             o_ref.at[:, cols[j]], *window(step - 2))


def _band_attn(layer, z, kt, bias_rows, band_mask, g_att, batch, seq):
    width = ATT_HEADS_PER_STEP * ATT_HD
    blk = lambda off: off // width
    b_h = lambda off: (lambda h, b: (b, off + h))
    return pl.pallas_call(
        _attn_kernel,
        grid=(ATT_HEADS // ATT_HEADS_PER_STEP, batch),
        in_specs=[
            pl.BlockSpec((seq, width), b_h(blk(Z_AQ))),
            pl.BlockSpec((width, seq), lambda h, b: (blk(KT_A) + h, b)),
            pl.BlockSpec((seq, width), b_h(blk(Z_AV))),
            pl.BlockSpec((seq, width), b_h(blk(Z_AG))),
            pl.BlockSpec((None, ATT_HEADS_PER_STEP, 1, ATT_TQ + ATT_WIN), lambda h, b: (layer, h, 0, 0)),
            pl.BlockSpec((ATT_TQ, ATT_WIN), lambda h, b: (0, 0)),
            pl.BlockSpec((None, 1, width), lambda h, b: (layer, 0, h)),
        ],
        out_specs=pl.BlockSpec((seq, width), lambda h, b: (b, h)),
        out_shape=jax.ShapeDtypeStruct((z.shape[0], D_ATT), jnp.bfloat16),
        scratch_shapes=[pltpu.VMEM((ATT_HEADS_PER_STEP, ATT_TQ, ATT_WIN), jnp.float32)],
        compiler_params=pltpu.CompilerParams(
            dimension_semantics=("arbitrary", "arbitrary"), vmem_limit_bytes=VMEM_LIMIT),
        name="band_attn",
    )(z, kt, z, z, bias_rows, band_mask, g_att)


def _band_bias_rows(rel_bias):
    lead = rel_bias.shape[:-1]
    rb = rel_bias.astype(jnp.float32) * LOG2E
    far = rb[..., 2 * REL_CLIP:]
    row = jnp.concatenate([jnp.broadcast_to(far, lead + (ATT_WIN - REL_CLIP - (ATT_TQ - 1),)),
                           rb[..., 2 * REL_CLIP - 1:0:-1],
                           jnp.zeros(lead + (1,), jnp.float32),
                           jnp.broadcast_to(far, lead + (ATT_TQ - 1,))], axis=-1)
    return row[..., None, :]


def _band_mask():
    i = np.arange(ATT_TQ)[:, None]
    j = np.arange(ATT_WIN)[None, :]
    first = (i // CHUNK) * CHUNK
    in_band = (j >= first) & (j < first + (LEFT_CHUNKS + 1) * CHUNK)
    return jnp.asarray(np.where(in_band, 0.0, -1e30), jnp.float32)


_GLA_ROWS = [0] + [2 * GLA_KW + t * PROJ_TN for t in range(2 * D_GLA // PROJ_TN)]
_ATT_ROWS = ([W_AQ + t * PROJ_TN for t in range(D_ATT // PROJ_TN)]
             + [W_AQ + 2 * D_ATT + t * PROJ_TN for t in range(2 * D_ATT // PROJ_TN)])
_NT_ROWS = [GLA_KW] + [W_AQ + D_ATT + t * PROJ_TN for t in range(D_ATT // PROJ_TN)]


def _table_lookup(table, j):
    val = table[0] + j * 0
    for t in range(1, len(table)):
        val = jnp.where(j >= t, table[t], val)
    return val


def _cast_rows(src_ref, dst_ref, n_rows):
    def body(r, carry):
        rows = pl.ds(pl.multiple_of(r * CAST_ROWS, CAST_ROWS), CAST_ROWS)
        dst_ref[rows, :] = src_ref[rows, :].astype(dst_ref.dtype)
        return carry
    lax.fori_loop(0, n_rows // CAST_ROWS, body, 0)


def _w_rows_spec(layer, table, n_rows, block_of):
    units = [(layer * D_IN + r) // ROW_ALIGN for r in table]
    return pl.BlockSpec((pl.Element(n_rows), pl.Element(D_MODEL)),
                        lambda s: (_table_lookup(units, block_of(s)) * ROW_ALIGN, 0))


def _proj_nn_kernel(n_units, hn_ref, wt_ref, *refs):
    if n_units:
        gla_in, (z_ref, og_ref, wbf_ref), gla_scratch = refs[:8], refs[8:11], refs[11:]
    else:
        z_ref, wbf_ref = refs
    s = pl.program_id(0)
    _cast_rows(wt_ref, wbf_ref, PROJ_TN)

    def matmul_piece(r0, n_rows, c0, n_cols):
        def run():
            rows, cols = slice(r0, r0 + n_rows), slice(c0, c0 + n_cols)
            z_ref[rows, cols] = lax.dot_general(hn_ref[rows, :], wbf_ref[cols, :], (((1,), (1,)), ((), ())),
                                                preferred_element_type=jnp.float32).astype(z_ref.dtype)
        return run

    matmul = matmul_piece(0, PROJ_TM, 0, PROJ_TN)

    if n_units:
        @pl.when(s < n_units)
        def _():
            pieces = [matmul_piece(r0, MIX_PIECE_ROWS, c0, MIX_PIECE_COLS)
                      for r0 in range(0, PROJ_TM, MIX_PIECE_ROWS) for c0 in range(0, PROJ_TN, MIX_PIECE_COLS)]
            _gla_unit(pieces, *gla_in, og_ref, *gla_scratch)

        pl.when(s >= n_units)(matmul)
    else:
        matmul()


def _proj_nn(layer, hn, w_in_t, table, gla=None):
    m = hn.shape[0]
    n_blk = len(table)
    steps = n_blk * (m // PROJ_TM)
    in_specs = [pl.BlockSpec((PROJ_TM, D_MODEL), lambda s: (s // n_blk, 0)),
                _w_rows_spec(layer, table, PROJ_TN, lambda s: s % n_blk)]
    out_specs = [pl.BlockSpec((PROJ_TM, PROJ_TN), lambda s: (s // n_blk, s % n_blk))]
    out_shape = [jax.ShapeDtypeStruct((m, len(table) * PROJ_TN), jnp.bfloat16)]
    scratch = [pltpu.VMEM((PROJ_TN, D_MODEL), jnp.bfloat16)]
    args = [hn, w_in_t]
    n_units = 0
    if gla is not None:
        zg, kt, gat, walt, bcol, g_gla, batch, seq = gla
        n_units = batch * GLA_HEADS
        assert n_units <= steps

        def unit(s):
            u = jnp.minimum(s, n_units - 1)
            return u // GLA_HEADS, u % GLA_HEADS

        g_in, g_out = _gla_specs(layer, seq, unit)
        in_specs += g_in
        out_specs.append(g_out)
        out_shape.append(jax.ShapeDtypeStruct((m, D_GLA), jnp.bfloat16))
        scratch += _gla_scratch(seq)
        args += [zg, kt, zg, zg, gat, walt, bcol, g_gla]
    return pl.pallas_call(
        functools.partial(_proj_nn_kernel, n_units),
        grid=(steps,),
        in_specs=in_specs,
        out_specs=out_specs,
        out_shape=out_shape,
        scratch_shapes=scratch,
        compiler_params=pltpu.CompilerParams(
            dimension_semantics=("arbitrary",),
            vmem_limit_bytes=VMEM_LIMIT_FUSED if n_units else VMEM_LIMIT),
        name="proj_att_gla" if n_units else "proj_gla",
    )(*args)


def _proj_nt_kernel(tiles, hn_ref, wt_ref, wga_ref, kt_ref, gat_ref, wbf_ref):
    s = pl.program_id(0)

    @pl.when(s % tiles == 0)
    def _():
        _cast_rows(wt_ref, wbf_ref, PROJ_TN)

    @pl.when(s == 0)
    def _():
        wbf_ref[PROJ_TN:, :] = wga_ref[...].astype(wbf_ref.dtype)

    res = lax.dot_general(wbf_ref[...], hn_ref[...], (((1,), (1,)), ((), ())),
                          preferred_element_type=jnp.float32)
    scale = jnp.where(s < tiles, GLA_DK ** -0.5, ATT_HD ** -0.5 * LOG2E)
    kt_ref[...] = (res[:PROJ_TN] * scale).astype(kt_ref.dtype)
    gat_ref[...] = res[PROJ_TN:].astype(gat_ref.dtype)


def _proj_nt(layer, hn, w_in_t):
    m = hn.shape[0]
    tiles = m // PROJ_TM
    return pl.pallas_call(
        functools.partial(_proj_nt_kernel, tiles),
        grid=(len(_NT_ROWS) * tiles,),
        in_specs=[pl.BlockSpec((PROJ_TM, D_MODEL), lambda s: (s % tiles, 0)),
                  _w_rows_spec(layer, _NT_ROWS, PROJ_TN, lambda s: s // tiles),
                  _w_rows_spec(layer, [W_GA], GATE_RANK, lambda s: 0)],
        out_specs=[pl.BlockSpec((PROJ_TN, PROJ_TM), lambda s: (s // tiles, s % tiles)),
                   pl.BlockSpec((None, GATE_RANK, PROJ_TM), lambda s: (s // tiles, 0, s % tiles))],
        out_shape=[jax.ShapeDtypeStruct((D_KT, m), jnp.bfloat16),
                   jax.ShapeDtypeStruct((len(_NT_ROWS), GATE_RANK, m), jnp.bfloat16)],
        scratch_shapes=[pltpu.VMEM((PROJ_TN + GATE_RANK, D_MODEL), jnp.bfloat16)],
        compiler_params=pltpu.CompilerParams(
            dimension_semantics=("arbitrary",), vmem_limit_bytes=VMEM_LIMIT),
        name="proj_nt",
    )(hn, w_in_t, w_in_t)


def _out_proj_kernel(fused, n_tiles, og_ref, oa_ref, w_ref, x_ref, gpost_ref, *rest):
    if fused:
        gpre_ref, h_ref, hn_ref, wbf_ref, *y_refs = rest
    else:
        h_ref, wbf_ref, *y_refs = rest
    i = pl.program_id(0)

    @pl.when(i == 0)
    def _():
        _cast_rows(w_ref, wbf_ref, D_GLA + D_ATT)

    def matmul(y_ref):
        y_ref[...] = (jnp.dot(og_ref[...], wbf_ref[:D_GLA, :], preferred_element_type=jnp.float32)
                      + jnp.dot(oa_ref[...], wbf_ref[D_GLA:, :], preferred_element_type=jnp.float32))

    def finish(y_ref):
        y = y_ref[...]
        h = x_ref[...] + y * _rms_scale(y) * gpost_ref[...]
        if fused:
            hn_ref[...] = (h * _rms_scale(h) * gpre_ref[...]).astype(hn_ref.dtype)
        h_ref[...] = h

    pl.when(i == 0)(lambda: matmul(y_refs[0]))
    for parity in range(2):
        @pl.when((i > 0) & (i < n_tiles) & (i % 2 == parity))
        def _():
            matmul(y_refs[parity])
            finish(y_refs[1 - parity])
    pl.when(i == n_tiles)(lambda: finish(y_refs[(n_tiles - 1) % 2]))


def _out_proj(layer, og, oa, w_out, x, g_post, g_pre=None):
    m = x.shape[0]
    fused = g_pre is not None
    n_tiles = m // OUT_TM
    mm_tile = lambda i: (jnp.minimum(i, n_tiles - 1), 0)
    fin_tile = lambda i: (jnp.maximum(i - 1, 0), 0)
    in_specs = [
        pl.BlockSpec((OUT_TM, D_GLA), mm_tile),
        pl.BlockSpec((OUT_TM, D_ATT), mm_tile),
        pl.BlockSpec((None, D_GLA + D_ATT, D_MODEL), lambda i: (layer, 0, 0), pipeline_mode=pl.Buffered(1)),
        pl.BlockSpec((OUT_TM, D_MODEL), fin_tile),
        pl.BlockSpec((None, 1, D_MODEL), lambda i: (layer, 0, 0)),
    ]
    out_specs = [pl.BlockSpec((OUT_TM, D_MODEL), fin_tile)]
    out_shape = [jax.ShapeDtypeStruct((m, D_MODEL), jnp.float32)]
    args = [og, oa, w_out, x, g_post]
    if fused:
        in_specs.append(pl.BlockSpec((None, 1, D_MODEL), lambda i: (layer + 1, 0, 0)))
        out_specs.append(pl.BlockSpec((OUT_TM, D_MODEL), fin_tile))
        out_shape.append(jax.ShapeDtypeStruct((m, D_MODEL), jnp.bfloat16))
        args.append(g_pre)
    return pl.pallas_call(
        functools.partial(_out_proj_kernel, fused, n_tiles),
        grid=(n_tiles + 1,),
        in_specs=in_specs,
        out_specs=out_specs,
        out_shape=out_shape,
        scratch_shapes=[pltpu.VMEM((D_GLA + D_ATT, D_MODEL), jnp.bfloat16),
                        pltpu.VMEM((OUT_TM, D_MODEL), jnp.float32),
                        pltpu.VMEM((OUT_TM, D_MODEL), jnp.float32)],
        compiler_params=pltpu.CompilerParams(
            dimension_semantics=("arbitrary",), vmem_limit_bytes=VMEM_LIMIT),
        name="out_proj",
    )(*args)


def kernel(x, w_in, w_out, g_pre, g_post, w_alpha, b_alpha, g_gla, g_att, rel_bias):
    batch, seq, d_model = x.shape
    depth = w_in.shape[0]
    h = x.reshape(batch * seq, d_model)
    w_in_t = jnp.swapaxes(w_in, 1, 2).reshape(depth * D_IN, d_model)
    walt = jnp.pad(jnp.swapaxes(w_alpha, 1, 2), ((0, 0), (0, 0), (0, LANES - GATE_RANK))).astype(jnp.bfloat16)
    bcol = b_alpha[:, :, None]
    bias_rows, band_mask = _band_bias_rows(rel_bias), _band_mask()
    row = lambda p: p[:, None, :]
    hn = _norm_call(0, h, row(g_pre))
    for l in range(depth):
        kt, gat = _proj_nt(l, hn, w_in_t)
        (zg,) = _proj_nn(l, hn, w_in_t, _GLA_ROWS)
        za, og = _proj_nn(l, hn, w_in_t, _ATT_ROWS, gla=(zg, kt, gat, walt, bcol, row(g_gla), batch, seq))
        oa = _band_attn(l, za, kt, bias_rows, band_mask, row(g_att), batch, seq)
        if l + 1 < depth:
            h, hn = _out_proj(l, og, oa, w_out, h, row(g_post), row(g_pre))
        else:
            (h,) = _out_proj(l, og, oa, w_out, h, row(g_post))
    return h.reshape(batch, seq, d_model)
```
